```python
import jax, jax.numpy as jnp
from jax import lax
import numpy as np

D_MODEL = 1024
BATCH = 8
SEQ = 8192
DEPTH = 1

D_RNN = 1280
RNN_BLOCKS = 10
RNN_BW = D_RNN // RNN_BLOCKS
CONV_W = 4
LRU_C = 8.0
N_HEADS = 8
HEAD_DIM = 128
D_ATT = N_HEADS * HEAD_DIM
KV_RANK = 256
IDX_HEADS = 8
IDX_DIM = 64
TOPK_MAX = 256
Q_BLOCK = 128
IDX_SCALE = IDX_DIM ** -0.5 * IDX_HEADS ** -0.5
ATT_SCALE = HEAD_DIM ** -0.5
N_GROUPS = 4
EXP_PER_GROUP = 8
N_EXPERTS = N_GROUPS * EXP_PER_GROUP
TOP_K_INNER = 2
D_EXPERT = 512
MOE_BLOCK = 128
N_BRANCH = 2
EPS = 1e-6

SPLITS = (D_RNN, D_RNN, D_ATT, KV_RANK, IDX_HEADS * IDX_DIM, IDX_DIM, IDX_HEADS, N_BRANCH * D_MODEL)
D_IN = sum(SPLITS)

kernel_name = 'hybrid_rglru_dsa_hmoe_block'


def rmsnorm(x, g):
    xf = x.astype(jnp.float32)
    y = xf * lax.rsqrt(jnp.mean(xf * xf, axis=-1, keepdims=True) + EPS)
    return (y * g.astype(jnp.float32)).astype(x.dtype)


def modulate(x, g, shift, scale):
    return rmsnorm(x, g) * (1 + scale[:, None, :]) + shift[:, None, :]


def causal_dwconv(x, w, b):
    y = lax.conv_general_dilated(x, w[:, None, :], window_strides=(1,), padding=[(CONV_W - 1, 0)],
                                 dimension_numbers=('NWC', 'WIO', 'NWC'), feature_group_count=x.shape[-1])
    return y + b


def rg_lru(x, w_a, b_a, w_x, b_x, lam):
    B_, S_, _ = x.shape
    xb = x.reshape(B_, S_, RNN_BLOCKS, RNN_BW)
    r = jax.nn.sigmoid(jnp.einsum('bsnc,ncd->bsnd', xb, w_a).reshape(B_, S_, D_RNN) + b_a)
    i = jax.nn.sigmoid(jnp.einsum('bsnc,ncd->bsnd', xb, w_x).reshape(B_, S_, D_RNN) + b_x)
    log_a = -LRU_C * jax.nn.softplus(-lam.astype(jnp.float32)) * r.astype(jnp.float32)
    a = jnp.exp(log_a)
    inp = jnp.sqrt(-jnp.expm1(2.0 * log_a)) * (i * x).astype(jnp.float32)

    def combine(lhs, rhs):
        a1, b1 = lhs
        a2, b2 = rhs
        return a1 * a2, a2 * b1 + b2

    _, h = lax.associative_scan(combine, (a, inp), axis=1)
    return h.astype(x.dtype)


def dsa_attention(q, c_kv, q_idx, k_idx, w_idx, w_uk, w_uv):
    B_, S_ = q.shape[0], q.shape[1]
    topk = min(TOPK_MAX, S_ // 4)
    nblk = S_ // Q_BLOCK
    slopes = 2.0 ** (-8.0 * jnp.arange(1, N_HEADS + 1, dtype=jnp.float32) / N_HEADS)
    key_pos = jnp.arange(S_)

    def to_blocks(t):
        return jnp.moveaxis(t.reshape((B_, nblk, Q_BLOCK) + t.shape[2:]), 1, 0)

    def one_block(args):
        qb, qib, wib, q0 = args
        qpos = q0 + jnp.arange(Q_BLOCK)
        causal = key_pos[None, :] <= qpos[:, None]
        rel = jax.nn.relu(jnp.einsum('bqhd,bsd->bqhs', qib, k_idx).astype(jnp.float32))
        score = jnp.einsum('bqh,bqhs->bqs', wib.astype(jnp.float32), rel) * IDX_SCALE
        score = jnp.where(causal[None], score, -jnp.inf)
        _, sel = lax.top_k(score, topk)
        c_sel = jax.vmap(lambda cb, ib: cb[ib])(c_kv, sel)
        q_lat = jnp.einsum('bqhd,rhd->bqhr', qb, w_uk)
        logits = jnp.einsum('bqhr,bqkr->bqhk', q_lat, c_sel).astype(jnp.float32) * ATT_SCALE
        dist = (qpos[None, :, None] - sel).astype(jnp.float32)
        logits = logits - slopes[None, None, :, None] * dist[:, :, None, :]
        valid = sel <= qpos[None, :, None]
        logits = jnp.where(valid[:, :, None, :], logits, -jnp.inf)
        p = jax.nn.softmax(logits, axis=-1).astype(c_sel.dtype)
        o_lat = jnp.einsum('bqhk,bqkr->bqhr', p, c_sel)
        return jnp.einsum('bqhr,rhd->bqhd', o_lat, w_uv)

    out = lax.map(one_block, (to_blocks(q), to_blocks(q_idx), to_blocks(w_idx), jnp.arange(nblk) * Q_BLOCK))
    return jnp.moveaxis(out, 0, 1).reshape(B_, S_, N_HEADS * HEAD_DIM)


def hier_moe(x, w_group, b_group, w_expert, b_expert, w1, w3, w2):
    B_, S_, D_ = x.shape
    T = B_ * S_
    xf = x.reshape(T, D_)
    g_logits = (xf @ w_group + b_group).astype(jnp.float32)
    g_sel = jnp.argmax(g_logits, axis=-1)
    g_w = jnp.take_along_axis(jax.nn.softmax(g_logits, axis=-1), g_sel[:, None], axis=-1)
    e_logits = (xf @ w_expert + b_expert).astype(jnp.float32).reshape(T, N_GROUPS, EXP_PER_GROUP)
    e_logits = jnp.take_along_axis(e_logits, g_sel[:, None, None], axis=1)[:, 0]
    e_top, e_loc = lax.top_k(e_logits, TOP_K_INNER)
    gate = (g_w * jax.nn.softmax(e_top, axis=-1)).reshape(-1)
    eid = (g_sel[:, None] * EXP_PER_GROUP + e_loc).reshape(-1)
    tok = jnp.repeat(jnp.arange(T, dtype=jnp.int32), TOP_K_INNER)
    A = T * TOP_K_INNER
    order = jnp.argsort(eid)
    eid_s = eid[order]
    counts = jnp.bincount(eid, length=N_EXPERTS)
    padded = (counts + MOE_BLOCK - 1) // MOE_BLOCK * MOE_BLOCK
    start = jnp.cumsum(counts) - counts
    pstart = jnp.cumsum(padded) - padded
    dest = pstart[eid_s] + jnp.arange(A) - start[eid_s]
    n_rows = A + N_EXPERTS * MOE_BLOCK
    n_blk = n_rows // MOE_BLOCK
    row_tok = jnp.zeros((n_rows,), jnp.int32).at[dest].set(tok[order])
    row_gate = jnp.zeros((n_rows,), x.dtype).at[dest].set(gate[order].astype(x.dtype))
    blk_exp = jnp.minimum(jnp.searchsorted(jnp.cumsum(padded), jnp.arange(n_blk) * MOE_BLOCK, side='right'),
                          N_EXPERTS - 1)
    xs = xf[row_tok].reshape(n_blk, MOE_BLOCK, D_)

    def expert_block(args):
        xb, e = args
        hid = jax.nn.silu(xb @ w1[e]) * (xb @ w3[e])
        return hid @ w2[e]

    ys = lax.map(expert_block, (xs, blk_exp)).reshape(n_rows, D_)
    out = jax.ops.segment_sum(ys * row_gate[:, None], row_tok, num_segments=T)
    return out.reshape(B_, S_, D_)


def _w(k, shape, fan_in, mult=1.0):
    return jax.random.normal(k, shape, jnp.float32) * (mult * fan_in ** -0.5)


def _small(k, shape):
    return 0.01 * jax.random.normal(k, shape, jnp.float32)


def _gain(k, shape):
    return 1.0 + 0.01 * jax.random.normal(k, shape, jnp.float32)


def setup_inputs(seed: int = 0) -> dict:
    key = jax.random.key(seed)
    ks = list(jax.random.split(key, 32))
    L = DEPTH
    u = jax.random.uniform(ks.pop(), (L, D_RNN), jnp.float32, minval=0.9, maxval=0.999)
    a0 = u ** (1.0 / LRU_C)
    lru_lambda = jnp.log(a0) - jnp.log1p(-a0)
    return {
        'x': jax.random.normal(ks.pop(), (BATCH, SEQ, D_MODEL), jnp.float32),
        'c': jax.random.normal(ks.pop(), (BATCH, D_MODEL), jnp.float32),
        'w_mod': _w(ks.pop(), (L, D_MODEL, 6 * D_MODEL), D_MODEL, 0.5),
        'b_mod': _small(ks.pop(), (L, 6 * D_MODEL)),
        'norm1_g': _gain(ks.pop(), (L, D_MODEL)),
        'w_in': _w(ks.pop(), (L, D_MODEL, D_IN), D_MODEL),
        'conv_w': _w(ks.pop(), (L, CONV_W, D_RNN), CONV_W),
        'conv_b': _small(ks.pop(), (L, D_RNN)),
        'w_rg_a': _w(ks.pop(), (L, RNN_BLOCKS, RNN_BW, RNN_BW), RNN_BW),
        'b_rg_a': _small(ks.pop(), (L, D_RNN)),
        'w_rg_x': _w(ks.pop(), (L, RNN_BLOCKS, RNN_BW, RNN_BW), RNN_BW),
        'b_rg_x': _small(ks.pop(), (L, D_RNN)),
        'lru_lambda': lru_lambda,
        'kv_norm_g': _gain(ks.pop(), (L, KV_RANK)),
        'w_uk': _w(ks.pop(), (L, KV_RANK, N_HEADS, HEAD_DIM), KV_RANK),
        'w_uv': _w(ks.pop(), (L, KV_RANK, N_HEADS, HEAD_DIM), KV_RANK),
        'w_rnn_out': _w(ks.pop(), (L, D_RNN, D_MODEL), D_RNN),
        'w_att_out': _w(ks.pop(), (L, D_ATT, D_MODEL), D_ATT),
        'w_out': _w(ks.pop(), (L, D_MODEL, D_MODEL), D_MODEL),
        'norm2_g': _gain(ks.pop(), (L, D_MODEL)),
        'w_group': _w(ks.pop(), (L, D_MODEL, N_GROUPS), D_MODEL),
        'b_group': _small(ks.pop(), (L, N_GROUPS)),
        'w_expert': _w(ks.pop(), (L, D_MODEL, N_EXPERTS), D_MODEL),
        'b_expert': _small(ks.pop(), (L, N_EXPERTS)),
        'moe_w1': _w(ks.pop(), (L, N_EXPERTS, D_MODEL, D_EXPERT), D_MODEL),
        'moe_w3': _w(ks.pop(), (L, N_EXPERTS, D_MODEL, D_EXPERT), D_MODEL),
        'moe_w2': _w(ks.pop(), (L, N_EXPERTS, D_EXPERT, D_MODEL), D_EXPERT),
        'final_g': _gain(ks.pop(), (D_MODEL,)),
    }


def reference(x, c, w_mod, b_mod, norm1_g, w_in, conv_w, conv_b, w_rg_a, b_rg_a, w_rg_x, b_rg_x,
              lru_lambda, kv_norm_g, w_uk, w_uv, w_rnn_out, w_att_out, w_out, norm2_g, w_group, b_group,
              w_expert, b_expert, moe_w1, moe_w3, moe_w2, final_g):
    B_, S_, _ = x.shape
    split_pts = [int(v) for v in np.cumsum(SPLITS)[:-1]]
    h = x
    for l in range(DEPTH):
        mod = jax.nn.silu(c) @ w_mod[l] + b_mod[l]
        sh1, sc1, g1, sh2, sc2, g2 = jnp.split(mod, 6, axis=-1)
        u = modulate(h, norm1_g[l], sh1, sc1)
        proj = u @ w_in[l]
        xr, gr, q, ckv, qi, ki, wi, mg = jnp.split(proj, split_pts, axis=-1)
        hr = rg_lru(causal_dwconv(xr, conv_w[l], conv_b[l]), w_rg_a[l], b_rg_a[l], w_rg_x[l], b_rg_x[l],
                    lru_lambda[l])
        y_a = (jax.nn.gelu(gr) * hr) @ w_rnn_out[l]
        att = dsa_attention(q.reshape(B_, S_, N_HEADS, HEAD_DIM), rmsnorm(ckv, kv_norm_g[l]),
                            qi.reshape(B_, S_, IDX_HEADS, IDX_DIM), ki, wi, w_uk[l], w_uv[l])
        y_b = att @ w_att_out[l]
        ga, gb = jnp.split(jax.nn.sigmoid(mg), N_BRANCH, axis=-1)
        mix = (ga * y_a + gb * y_b) @ w_out[l]
        h = h + g1[:, None, :] * mix
        u2 = modulate(h, norm2_g[l], sh2, sc2)
        h = h + g2[:, None, :] * hier_moe(u2, w_group[l], b_group[l], w_expert[l], b_expert[l],
                                          moe_w1[l], moe_w3[l], moe_w2[l])
    return rmsnorm(h, final_g)
```

```python
import functools

import jax
import jax.numpy as jnp
from jax import lax
from jax.experimental import pallas as pl
from jax.experimental.pallas import tpu as pltpu
from jax.experimental.pallas import tpu_sc as plsc

LRU_C = 8.0
CONV_W = 4
N_HEADS = 8
IDX_HEADS = 8
TOPK_MAX = 256
N_GROUPS = 4
EXP_PER_GROUP = 8
EPS = 1e-6

LANES = 128
SUBLANES = 8
SC_LANES = 16
VMEM_LIMIT = 56 * 1024 * 1024

Q_TILE = 128
KEY_TILE = 256
SC_CHUNK = 64
INT_MIN = -(2 ** 31)

F32 = jnp.float32
BF16 = jnp.bfloat16
I32 = jnp.int32
U32 = jnp.uint32


def _cparams(sem):
    return pltpu.CompilerParams(dimension_semantics=sem, vmem_limit_bytes=VMEM_LIMIT)


def _dot(a, b, **kw):
    return jnp.dot(a, b, preferred_element_type=F32, **kw)


def _dot_nt(a, b):
    return lax.dot_general(a, b, (((1,), (1,)), ((), ())), preferred_element_type=F32)


def _rms(xf, g):
    return xf * lax.rsqrt(jnp.mean(xf * xf, axis=-1, keepdims=True) + EPS) * g


def _mod_kernel(c_ref, w_ref, b_ref, o_ref):
    c = c_ref[...]
    o_ref[...] = _dot(jax.nn.silu(c), w_ref[...], precision=lax.Precision.HIGHEST) + b_ref[...]


def _adaln(c, w_mod, b_mod):
    B, D = c.shape
    N = w_mod.shape[1]
    tn = 1024
    return pl.pallas_call(
        _mod_kernel,
        out_shape=jax.ShapeDtypeStruct((B, N), F32),
        grid=(N // tn,),
        in_specs=[pl.BlockSpec((B, D), lambda j: (0, 0)),
                  pl.BlockSpec((D, tn), lambda j: (0, j)),
                  pl.BlockSpec((1, tn), lambda j: (0, j))],
        out_specs=pl.BlockSpec((B, tn), lambda j: (0, j)),
        compiler_params=_cparams(("arbitrary",)),
        name="adaln",
    )(c, w_mod, b_mod.reshape(1, N))


def _inproj_kernel(x_ref, sh_ref, sc_ref, g_ref, wxg_ref, wq_ref, wc_ref, wqi_ref, wki_ref, wwi_ref,
                   wmg_ref, wuk_ref, kvg_ref,
                   xr_ref, gr_ref, ql_ref, ct_ref, qi_ref, ki_ref, wi_ref, ga_ref, gb_ref, *, d_rnn, kv_rank, idx_dim):
    x = x_ref[0]
    u = _rms(x, g_ref[...]) * (1.0 + sc_ref[0]) + sh_ref[0]
    ub = u.astype(BF16)
    xg = _dot(ub, wxg_ref[...])
    xr_ref[0] = xg[:, :d_rnn].astype(BF16)
    gr_ref[0] = xg[:, d_rnn:].astype(BF16)
    q = _dot(ub, wq_ref[...]).astype(BF16)
    hd = q.shape[1] // N_HEADS
    for h in range(N_HEADS):
        ql_ref[0, :, h * kv_rank:(h + 1) * kv_rank] = _dot(q[:, h * hd:(h + 1) * hd], wuk_ref[h]).astype(BF16)
    cn = _rms(_dot(ub, wc_ref[...]), kvg_ref[...])
    half = kv_rank // 2
    lo = lax.bitcast_convert_type(cn[:, :half].astype(BF16).astype(F32), U32)
    hi = lax.bitcast_convert_type(cn[:, half:].astype(BF16).astype(F32), U32)
    ct_ref[0] = (lo >> 16) | (hi & jnp.uint32(0xFFFF0000))
    qi = _dot(ub, wqi_ref[...]).astype(BF16)
    for h in range(IDX_HEADS):
        qi_ref[0, h] = qi[:, h * idx_dim:(h + 1) * idx_dim]
    ki_ref[0] = _dot(ub, wki_ref[...])[:, :idx_dim].astype(BF16)
    wi_ref[0] = _dot_nt(wwi_ref[...], ub)
    mg = jax.nn.sigmoid(_dot(ub, wmg_ref[...]))
    dm = mg.shape[1] // 2
    ga_ref[0] = mg[:, :dm].astype(BF16)
    gb_ref[0] = mg[:, dm:].astype(BF16)


def _inproj(x, sh1, sc1, norm1_g, w_in, w_uk, kv_norm_g, dims):
    B, S, D = x.shape
    d_rnn, d_att, kv_rank, idx_dim = dims
    tm = min(512, S)
    o = 0
    wxg = w_in[:, o:o + 2 * d_rnn].astype(BF16); o += 2 * d_rnn
    wq = w_in[:, o:o + d_att].astype(BF16); o += d_att
    wc = w_in[:, o:o + kv_rank].astype(BF16); o += kv_rank
    wqi = w_in[:, o:o + IDX_HEADS * idx_dim].astype(BF16); o += IDX_HEADS * idx_dim
    wki = jnp.pad(w_in[:, o:o + idx_dim], ((0, 0), (0, LANES - idx_dim))).astype(BF16); o += idx_dim
    wwi = w_in[:, o:o + IDX_HEADS].T.astype(BF16); o += IDX_HEADS
    wmg = w_in[:, o:].astype(BF16)
    hd = d_att // N_HEADS
    wuk = jnp.transpose(w_uk, (1, 2, 0)).astype(BF16)
    const2 = lambda b, t: (0, 0)
    const3 = lambda b, t: (0, 0, 0)
    row = lambda b, t: (b, t, 0)
    per_b = lambda b, t: (b, 0, 0)
    out_shapes = (
        jax.ShapeDtypeStruct((B, S, d_rnn), BF16),
        jax.ShapeDtypeStruct((B, S, d_rnn), BF16),
        jax.ShapeDtypeStruct((B, S, N_HEADS * kv_rank), BF16),
        jax.ShapeDtypeStruct((B, S, kv_rank // 2), U32),
        jax.ShapeDtypeStruct((B, IDX_HEADS, S, idx_dim), BF16),
        jax.ShapeDtypeStruct((B, S, idx_dim), BF16),
        jax.ShapeDtypeStruct((B, IDX_HEADS, S), F32),
        jax.ShapeDtypeStruct((B, S, D), BF16),
        jax.ShapeDtypeStruct((B, S, D), BF16),
    )
    out_specs = (
        pl.BlockSpec((1, tm, d_rnn), row),
        pl.BlockSpec((1, tm, d_rnn), row),
        pl.BlockSpec((1, tm, N_HEADS * kv_rank), row),
        pl.BlockSpec((1, tm, kv_rank // 2), row),
        pl.BlockSpec((1, IDX_HEADS, tm, idx_dim), lambda b, t: (b, 0, t, 0)),
        pl.BlockSpec((1, tm, idx_dim), row),
        pl.BlockSpec((1, IDX_HEADS, tm), lambda b, t: (b, 0, t)),
        pl.BlockSpec((1, tm, D), row),
        pl.BlockSpec((1, tm, D), row),
    )
    in_specs = [
        pl.BlockSpec((1, tm, D), row),
        pl.BlockSpec((1, 1, D), per_b),
        pl.BlockSpec((1, 1, D), per_b),
        pl.BlockSpec((1, D), const2),
        pl.BlockSpec(wxg.shape, const2),
        pl.BlockSpec(wq.shape, const2),
        pl.BlockSpec(wc.shape, const2),
        pl.BlockSpec(wqi.shape, const2),
        pl.BlockSpec(wki.shape, const2),
        pl.BlockSpec(wwi.shape, const2),
        pl.BlockSpec(wmg.shape, const2),
        pl.BlockSpec(wuk.shape, const3),
        pl.BlockSpec((1, kv_rank), const2),
    ]
    return pl.pallas_call(
        functools.partial(_inproj_kernel, d_rnn=d_rnn, kv_rank=kv_rank, idx_dim=idx_dim),
        out_shape=out_shapes,
        grid=(B, S // tm),
        in_specs=in_specs,
        out_specs=out_specs,
        compiler_params=_cparams(("parallel", "parallel")),
        name="inproj",
    )(x, sh1[:, None, :], sc1[:, None, :], norm1_g.reshape(1, D), wxg, wq, wc, wqi, wki, wwi, wmg, wuk,
      kv_norm_g.reshape(1, kv_rank))


def _rglru_kernel(xr_ref, gr_ref, cw_ref, cb_ref, wa_ref, ba_ref, wx_ref, bx_ref, lam_ref, o_ref,
                  tail_ref, h_ref):
    t = pl.program_id(2)

    @pl.when(t == 0)
    def _():
        tail_ref[...] = jnp.zeros_like(tail_ref)
        h_ref[...] = jnp.zeros_like(h_ref)

    x = xr_ref[0].astype(F32)
    tt = x.shape[0]
    row = lax.broadcasted_iota(I32, x.shape, 0)
    row8 = lax.broadcasted_iota(I32, tail_ref.shape, 0)
    tail = tail_ref[...]
    cw = cw_ref[...]
    xc = x * cw[CONV_W - 1:CONV_W, :] + cb_ref[...]
    for d in range(1, CONV_W):
        rolled = pltpu.roll(x, d, 0)
        head = jnp.where(row8 < d, pltpu.roll(tail, d, 0), rolled[:SUBLANES])
        xs = jnp.concatenate([head, rolled[SUBLANES:]], axis=0)
        xc = xc + xs * cw[CONV_W - 1 - d:CONV_W - d, :]
    tail_ref[...] = x[tt - SUBLANES:]
    xb = xc.astype(BF16)
    r = jax.nn.sigmoid(_dot(xb, wa_ref[0]) + ba_ref[...])
    i = jax.nn.sigmoid(_dot(xb, wx_ref[0]) + bx_ref[...])
    log_a = (-LRU_C * jax.nn.softplus(-lam_ref[...])) * r
    a = jnp.exp(log_a)
    b = jnp.sqrt(1.0 - a * a) * (i * xc)
    d = 1
    while d < tt:
        a_s = jnp.where(row >= d, pltpu.roll(a, d, 0), 1.0)
        b_s = jnp.where(row >= d, pltpu.roll(b, d, 0), 0.0)
        b = a * b_s + b
        a = a * a_s
        d *= 2
    h = a * h_ref[...] + b
    h_ref[...] = h[tt - 1:tt, :]
    o_ref[0] = (jax.nn.gelu(gr_ref[0].astype(F32)) * h).astype(BF16)


def _rglru(xr, gr, conv_w, conv_b, w_a, b_a, w_x, b_x, lam):
    B, S, C = xr.shape
    nb, bw = w_a.shape[0], w_a.shape[1]
    tt = min(256, S)
    blk = lambda b, n, t: (b, t, n)
    vec = lambda b, n, t: (0, n)
    mat = lambda b, n, t: (n, 0, 0)
    return pl.pallas_call(
        _rglru_kernel,
        out_shape=jax.ShapeDtypeStruct((B, S, C), BF16),
        grid=(B, nb, S // tt),
        in_specs=[pl.BlockSpec((1, tt, bw), blk), pl.BlockSpec((1, tt, bw), blk),
                  pl.BlockSpec((CONV_W, bw), vec), pl.BlockSpec((1, bw), vec),
                  pl.BlockSpec((1, bw, bw), mat), pl.BlockSpec((1, bw), vec),
                  pl.BlockSpec((1, bw, bw), mat), pl.BlockSpec((1, bw), vec),
                  pl.BlockSpec((1, bw), vec)],
        out_specs=pl.BlockSpec((1, tt, bw), blk),
        scratch_shapes=[pltpu.VMEM((SUBLANES, bw), F32), pltpu.VMEM((1, bw), F32)],
        compiler_params=_cparams(("parallel", "parallel", "arbitrary")),
        name="rglru",
    )(xr, gr, conv_w, conv_b.reshape(1, C), w_a.astype(BF16), b_a.reshape(1, C), w_x.astype(BF16),
      b_x.reshape(1, C), lam.reshape(1, C))


def _sortable(x):
    b = lax.bitcast_convert_type(x, I32)
    return b ^ ((b >> 31) & jnp.int32(0x7FFFFFFF))


def _indexer_kernel(qi_ref, ki_ref, wi_ref, sc_ref, par_ref, *, topk, idx_scale):
    i = pl.program_id(1)
    n_keys = (i + 1) * Q_TILE
    n_tiles = (n_keys + KEY_TILE - 1) // KEY_TILE
    qi = qi_ref[0].reshape(IDX_HEADS * Q_TILE, qi_ref.shape[-1])
    w = wi_ref[0]
    qpos = i * Q_TILE + lax.broadcasted_iota(I32, (1, Q_TILE), 1)
    rows = lax.broadcasted_iota(I32, (KEY_TILE, Q_TILE), 0)

    def score_tile(kt, carry):
        k0 = pl.multiple_of(kt * KEY_TILE, KEY_TILE)
        rel = jnp.maximum(_dot_nt(ki_ref[0, pl.ds(k0, KEY_TILE), :], qi), 0.0)
        s = rel[:, :Q_TILE] * w[0:1, :]
        for h in range(1, IDX_HEADS):
            s = s + rel[:, h * Q_TILE:(h + 1) * Q_TILE] * w[h:h + 1, :]
        s = s * idx_scale
        s = jnp.where(rows + k0 <= qpos, s, -jnp.inf)
        sc_ref[0, 0, pl.ds(k0, KEY_TILE), :] = _sortable(s)
        return carry

    lax.fori_loop(0, n_tiles, score_tile, 0)

    def count(pred):
        def body(c, acc):
            r0 = pl.multiple_of(c * KEY_TILE, KEY_TILE)
            x = sc_ref[0, 0, pl.ds(r0, KEY_TILE), :]
            m = pred(x, r0).astype(I32)
            return acc + jnp.sum(m.reshape(KEY_TILE // SUBLANES, SUBLANES, Q_TILE), axis=0)
        acc = lax.fori_loop(0, n_tiles, body, jnp.zeros((SUBLANES, Q_TILE), I32))
        return jnp.sum(acc, axis=0, keepdims=True)

    def bit_step(j, t):
        bit = 31 - j
        cand = jnp.where(bit == 31, jnp.zeros_like(t), t + (jnp.int32(1) << jnp.minimum(bit, 30)))
        ok = count(lambda x, r0: x >= cand) >= topk
        return jnp.where(ok, cand, t)

    tau = lax.fori_loop(0, 32, bit_step, jnp.full((1, Q_TILE), INT_MIN, I32))
    n_gt = count(lambda x, r0: x > tau)
    n_eq = count(lambda x, r0: x == tau)
    need = topk - n_gt
    short = qpos + 1 <= topk
    excess = jnp.logical_and(n_eq > need, jnp.logical_not(short))
    s_total = sc_ref.shape[2]

    def cut_search(_):
        def step(j, c):
            bit = jnp.int32(1) << (jnp.int32(s_total.bit_length() - 1) - j)
            cand = c + bit
            below = count(lambda x, r0: jnp.logical_and(x == tau, rows + r0 < cand))
            return jnp.where(below < need, cand, c)
        return lax.fori_loop(0, s_total.bit_length(), step, jnp.zeros((1, Q_TILE), I32))

    any_excess = jnp.max(excess.astype(I32)) > 0
    cut = lax.cond(any_excess, cut_search, lambda _: jnp.full((1, Q_TILE), s_total, I32), 0)
    cut = jnp.where(excess, cut, s_total)
    neg_inf_key = _sortable(jnp.full((1, Q_TILE), -jnp.inf, F32))
    tau = jnp.where(short, neg_inf_key, tau)
    cut = jnp.where(short, -1, cut)
    par_ref[0, 0] = jnp.concatenate([tau, cut, jnp.zeros((SUBLANES - 2, Q_TILE), I32)], axis=0)


def _indexer(qi, ki, wi, topk, idx_scale):
    B, H, S, di = qi.shape
    nq = S // Q_TILE
    s_pad = -(-S // KEY_TILE) * KEY_TILE
    if s_pad != S:
        ki = jnp.pad(ki, ((0, 0), (0, s_pad - S), (0, 0)))
    return pl.pallas_call(
        functools.partial(_indexer_kernel, topk=topk, idx_scale=idx_scale),
        out_shape=(jax.ShapeDtypeStruct((B, nq, s_pad, Q_TILE), I32),
                   jax.ShapeDtypeStruct((B, nq, SUBLANES, Q_TILE), I32)),
        grid=(B, nq),
        in_specs=[pl.BlockSpec((1, H, Q_TILE, di), lambda b, i: (b, 0, i, 0)),
                  pl.BlockSpec((1, s_pad, di), lambda b, i: (b, 0, 0)),
                  pl.BlockSpec((1, H, Q_TILE), lambda b, i: (b, 0, i))],
        out_specs=(pl.BlockSpec((1, 1, s_pad, Q_TILE), lambda b, i: (b, i, 0, 0)),
                   pl.BlockSpec((1, 1, SUBLANES, Q_TILE), lambda b, i: (b, i, 0, 0))),
        compiler_params=_cparams(("parallel", "arbitrary")),
        name="indexer",
    )(qi, ki, wi)


def _select_gather(scores, params, table, topk):
    B, nq, s_pad, _ = scores.shape
    W = table.shape[1]
    S = table.shape[0] // B
    info = plsc.get_sparse_core_info()
    nc, ns = info.num_cores, info.num_subcores
    nw = nc * ns
    n_units = B * nq
    assert n_units % nw == 0 and topk % SC_LANES == 0 and Q_TILE % SC_CHUNK == 0
    units_per_w = n_units // nw
    n_groups = Q_TILE // SC_LANES
    half = min(topk, 128)
    scores2 = scores.reshape(n_units, s_pad * Q_TILE)
    params2 = params.reshape(n_units, SUBLANES * Q_TILE)
    mesh = plsc.VectorSubcoreMesh(core_axis_name="c", subcore_axis_name="s")

    @functools.partial(
        pl.kernel, mesh=mesh,
        out_type=(jax.ShapeDtypeStruct((n_units, Q_TILE * topk), I32),
                  jax.ShapeDtypeStruct((n_units * Q_TILE, topk, W), U32)),
        scratch_types=[
            pltpu.VMEM((SC_CHUNK * Q_TILE,), I32),
            pltpu.VMEM((SUBLANES * Q_TILE,), I32),
            pltpu.VMEM((Q_TILE * topk,), I32),
            pltpu.VMEM((topk, W), U32),
            pltpu.SemaphoreType.DMA,
        ],
        compiler_params=pltpu.CompilerParams(needs_layout_passes=False),
        name="select_gather",
    )
    def k(sc_hbm, par_hbm, tab_hbm, sel_hbm, out_hbm, buf_v, par_v, idx_v, rows_v, sem):
        wid = lax.axis_index("s") * nc + lax.axis_index("c")
        lane = lax.iota(I32, SC_LANES)

        @pl.loop(0, units_per_w)
        def _(uu):
            unit = uu * nw + wid
            b = unit // nq
            i = unit - b * nq
            base = b * S
            pltpu.sync_copy(par_hbm.at[unit], par_v)
            fill = jnp.zeros((SC_LANES,), I32) + base

            @pl.loop(0, Q_TILE * topk // SC_LANES)
            def _(j):
                idx_v[pl.ds(j * SC_LANES, SC_LANES)] = fill

            taus = [par_v[pl.ds(g * SC_LANES, SC_LANES)] for g in range(n_groups)]
            cuts = [par_v[pl.ds(Q_TILE + g * SC_LANES, SC_LANES)] for g in range(n_groups)]
            slot0 = [(lane + g * SC_LANES) * topk for g in range(n_groups)]
            n_chunks = (i + 1) * (Q_TILE // SC_CHUNK)

            def chunk_body(c, cnts):
                pltpu.sync_copy(sc_hbm.at[unit, pl.ds(c * (SC_CHUNK * Q_TILE), SC_CHUNK * Q_TILE)], buf_v)

                def row_body(r, cnts):
                    key = c * SC_CHUNK + r
                    out = []
                    for g in range(n_groups):
                        s = buf_v[pl.ds(r * Q_TILE + g * SC_LANES, SC_LANES)]
                        m = (s > taus[g]) | ((s == taus[g]) & (key <= cuts[g]))
                        m = m & (cnts[g] < topk)
                        plsc.store_scatter(idx_v, [slot0[g] + cnts[g]], jnp.zeros((SC_LANES,), I32) + (key + base), mask=m)
                        out.append(cnts[g] + m.astype(I32))
                    return tuple(out)

                return lax.fori_loop(0, SC_CHUNK, row_body, cnts)

            zero = jnp.zeros((SC_LANES,), I32)
            lax.fori_loop(0, n_chunks, chunk_body, tuple(zero for _ in range(n_groups)))
            pltpu.sync_copy(idx_v, sel_hbm.at[unit])

            @pl.loop(0, Q_TILE)
            def _(q):
                copies = []
                for p in range(topk // half):
                    copies.append(pltpu.async_copy(
                        tab_hbm.at[idx_v.at[pl.ds(q * topk + p * half, half)]],
                        rows_v.at[pl.ds(p * half, half)], sem))
                for cp in copies:
                    cp.wait()
                pltpu.sync_copy(rows_v, out_hbm.at[unit * Q_TILE + q])

    return k(scores2, params2, table)


def _attn_kernel(cs_ref, sel_ref, ql_ref, sl_ref, o_ref, *, seq, topk, att_scale):
    tq = cs_ref.shape[0]
    t0 = pl.program_id(0) * tq
    w = cs_ref[...]
    lo = lax.bitcast_convert_type(w << 16, F32).astype(BF16)
    hi = lax.bitcast_convert_type(w & jnp.uint32(0xFFFF0000), F32).astype(BF16)
    g = jnp.concatenate([lo, hi], axis=-1)
    r = g.shape[-1]
    ql = ql_ref[...].reshape(tq, N_HEADS, r)
    logits = jnp.einsum("qhr,qkr->qhk", ql, g, preferred_element_type=F32) * att_scale
    row = t0 + lax.broadcasted_iota(I32, (tq, 1, 1), 0)
    base = (row // seq) * seq
    qpos = row - base
    sel = sel_ref[...].reshape(tq, 1, topk) - base
    logits = logits - sl_ref[...].reshape(1, N_HEADS, 1) * (qpos - sel).astype(F32)
    slot = lax.broadcasted_iota(I32, (1, 1, topk), 2)
    logits = jnp.where(slot < jnp.minimum(qpos + 1, topk), logits, -jnp.inf)
    p = jax.nn.softmax(logits, axis=-1)
    o = jnp.einsum("qhk,qkr->qhr", p.astype(BF16), g, preferred_element_type=F32)
    o_ref[...] = o.reshape(tq, N_HEADS * r).astype(BF16)


def _sparse_attn(csel, sel, qlat, seq, topk, att_scale):
    T, _, wh = csel.shape
    tq = 16
    hr = qlat.shape[1]
    slopes = (2.0 ** (-8.0 * jnp.arange(1, N_HEADS + 1, dtype=F32) / N_HEADS)).reshape(N_HEADS, 1)
    return pl.pallas_call(
        functools.partial(_attn_kernel, seq=seq, topk=topk, att_scale=att_scale),
        out_shape=jax.ShapeDtypeStruct((T, hr), BF16),
        grid=(T // tq,),
        in_specs=[pl.BlockSpec((tq, topk, wh), lambda i: (i, 0, 0)),
                  pl.BlockSpec((tq, topk), lambda i: (i, 0)),
                  pl.BlockSpec((tq, hr), lambda i: (i, 0)),
                  pl.BlockSpec((N_HEADS, 1), lambda i: (0, 0))],
        out_specs=pl.BlockSpec((tq, hr), lambda i: (i, 0)),
        compiler_params=_cparams(("parallel",)),
        name="sparse_attn",
    )(csel, sel, qlat, slopes)


def _merge_kernel(ol_ref, hg_ref, ga_ref, gb_ref, x_ref, g1_ref, sh_ref, sc_ref, n2_ref, wuv_ref, wao_ref,
                  wro_ref, wo_ref, wrt_ref, brt_ref, h1_ref, u2_ref, gate_ref):
    ol = ol_ref[0]
    r = wuv_ref.shape[1]
    att = jnp.concatenate([_dot(ol[:, h * r:(h + 1) * r], wuv_ref[h]) for h in range(N_HEADS)], axis=-1)
    y_b = _dot(att.astype(BF16), wao_ref[...])
    y_a = _dot(hg_ref[0], wro_ref[...])
    mixin = ga_ref[0].astype(F32) * y_a + gb_ref[0].astype(F32) * y_b
    mix = _dot(mixin.astype(BF16), wo_ref[...])
    h1 = x_ref[0] + g1_ref[0] * mix
    h1_ref[0] = h1
    u2 = _rms(h1, n2_ref[...]) * (1.0 + sc_ref[0]) + sh_ref[0]
    u2_ref[0] = u2.astype(BF16)
    lg = _dot(u2, wrt_ref[...], precision=lax.Precision.HIGHEST) + brt_ref[...]
    n_exp = N_GROUPS * EXP_PER_GROUP
    lane = lax.broadcasted_iota(I32, lg.shape, 1)
    big = jnp.int32(LANES)
    is_g = jnp.logical_and(lane >= n_exp, lane < n_exp + N_GROUPS)
    gl = jnp.where(is_g, lg, -jnp.inf)
    gmax = jnp.max(gl, axis=-1, keepdims=True)
    g_sel = jnp.min(jnp.where(jnp.logical_and(is_g, gl == gmax), lane, big), axis=-1, keepdims=True) - n_exp
    g_w = 1.0 / jnp.sum(jnp.where(is_g, jnp.exp(gl - gmax), 0.0), axis=-1, keepdims=True)
    in_grp = jnp.logical_and(lane >= g_sel * EXP_PER_GROUP, lane < (g_sel + 1) * EXP_PER_GROUP)
    el = jnp.where(in_grp, lg, -jnp.inf)
    e1 = jnp.max(el, axis=-1, keepdims=True)
    i1 = jnp.min(jnp.where(jnp.logical_and(in_grp, el == e1), lane, big), axis=-1, keepdims=True)
    el2 = jnp.where(lane == i1, -jnp.inf, el)
    e2 = jnp.max(el2, axis=-1, keepdims=True)
    i2 = jnp.min(jnp.where(jnp.logical_and(in_grp, el2 == e2), lane, big), axis=-1, keepdims=True)
    x2 = jnp.exp(e2 - e1)
    den = 1.0 + x2
    gate_ref[0] = jnp.where(lane == i1, g_w * (1.0 / den), 0.0) + jnp.where(lane == i2, g_w * (x2 / den), 0.0)


def _merge(olat, hg, ga, gb, x, g1, sh2, sc2, norm2_g, w_uv, w_att_out, w_rnn_out, w_out, w_group, b_group,
           w_expert, b_expert):
    B, S, D = x.shape
    tm = min(512, S)
    n_exp = w_expert.shape[1]
    wuv = jnp.transpose(w_uv, (1, 0, 2)).astype(BF16)
    wrt = jnp.pad(jnp.concatenate([w_expert, w_group], axis=1), ((0, 0), (0, LANES - n_exp - N_GROUPS)))
    brt = jnp.pad(jnp.concatenate([b_expert, b_group]), (0, LANES - n_exp - N_GROUPS)).reshape(1, LANES)
    row = lambda b, t: (b, t, 0)
    per_b = lambda b, t: (b, 0, 0)
    const2 = lambda b, t: (0, 0)
    const3 = lambda b, t: (0, 0, 0)
    olat = olat.reshape(B, S, -1)
    wao, wro, wo = w_att_out.astype(BF16), w_rnn_out.astype(BF16), w_out.astype(BF16)
    return pl.pallas_call(
        _merge_kernel,
        out_shape=(jax.ShapeDtypeStruct((B, S, D), F32), jax.ShapeDtypeStruct((B, S, D), BF16),
                   jax.ShapeDtypeStruct((B, S, LANES), F32)),
        grid=(B, S // tm),
        in_specs=[pl.BlockSpec((1, tm, olat.shape[-1]), row), pl.BlockSpec((1, tm, hg.shape[-1]), row),
                  pl.BlockSpec((1, tm, D), row), pl.BlockSpec((1, tm, D), row), pl.BlockSpec((1, tm, D), row),
                  pl.BlockSpec((1, 1, D), per_b), pl.BlockSpec((1, 1, D), per_b), pl.BlockSpec((1, 1, D), per_b),
                  pl.BlockSpec((1, D), const2), pl.BlockSpec(wuv.shape, const3), pl.BlockSpec(wao.shape, const2),
                  pl.BlockSpec(wro.shape, const2), pl.BlockSpec(wo.shape, const2), pl.BlockSpec(wrt.shape, const2),
                  pl.BlockSpec((1, LANES), const2)],
        out_specs=(pl.BlockSpec((1, tm, D), row), pl.BlockSpec((1, tm, D), row), pl.BlockSpec((1, tm, LANES), row)),
        compiler_params=_cparams(("parallel", "parallel")),
        name="merge",
    )(olat, hg, ga, gb, x, g1[:, None, :], sh2[:, None, :], sc2[:, None, :], norm2_g.reshape(1, D), wuv, wao, wro,
      wo, wrt, brt)


def _moe_kernel(u_ref, gate_ref, h1_ref, g2_ref, fg_ref, w1_ref, w3_ref, w2_ref, o_ref, acc_ref, *, final_norm):
    e = pl.program_id(2)

    @pl.when(e == 0)
    def _():
        acc_ref[...] = jnp.zeros_like(acc_ref)

    u = u_ref[0]
    hid = jax.nn.silu(_dot(u, w1_ref[0])) * _dot(u, w3_ref[0])
    y = _dot(hid.astype(BF16), w2_ref[0])
    gates = gate_ref[0]
    lane = lax.broadcasted_iota(I32, gates.shape, 1)
    gcol = jnp.sum(jnp.where(lane == e, gates, 0.0), axis=-1, keepdims=True)
    acc_ref[...] += y * gcol

    @pl.when(e == pl.num_programs(2) - 1)
    def _():
        h2 = h1_ref[0] + g2_ref[0] * acc_ref[...]
        o_ref[0] = _rms(h2, fg_ref[...]) if final_norm else h2


def _moe(u2, gates, h1, g2, final_g, w1, w3, w2, final_norm):
    B, S, D = h1.shape
    n_exp, _, de = w1.shape
    tm = min(1024, S)
    row = lambda b, t, e: (b, t, 0)
    return pl.pallas_call(
        functools.partial(_moe_kernel, final_norm=final_norm),
        out_shape=jax.ShapeDtypeStruct((B, S, D), F32),
        grid=(B, S // tm, n_exp),
        in_specs=[pl.BlockSpec((1, tm, D), row), pl.BlockSpec((1, tm, LANES), row), pl.BlockSpec((1, tm, D), row),
                  pl.BlockSpec((1, 1, D), lambda b, t, e: (b, 0, 0)), pl.BlockSpec((1, D), lambda b, t, e: (0, 0)),
                  pl.BlockSpec((1, D, de), lambda b, t, e: (e, 0, 0)),
                  pl.BlockSpec((1, D, de), lambda b, t, e: (e, 0, 0)),
                  pl.BlockSpec((1, de, D), lambda b, t, e: (e, 0, 0))],
        out_specs=pl.BlockSpec((1, tm, D), row),
        scratch_shapes=[pltpu.VMEM((tm, D), F32)],
        compiler_params=_cparams(("parallel", "parallel", "arbitrary")),
        name="moe",
    )(u2, gates, h1, g2[:, None, :], final_g.reshape(1, D), w1.astype(BF16), w3.astype(BF16), w2.astype(BF16))


def kernel(x, c, w_mod, b_mod, norm1_g, w_in, conv_w, conv_b, w_rg_a, b_rg_a, w_rg_x, b_rg_x, lru_lambda, kv_norm_g, w_uk, w_uv, w_rnn_out, w_att_out, w_out, norm2_g, w_group, b_group, w_expert, b_expert, moe_w1, moe_w3, moe_w2, final_g):
    B, S, D = x.shape
    depth = w_mod.shape[0]
    d_rnn = conv_w.shape[2]
    kv_rank, _, head_dim = w_uk.shape[1:]
    d_att = N_HEADS * head_dim
    idx_dim = (w_in.shape[2] - 2 * d_rnn - d_att - kv_rank - IDX_HEADS - 2 * D) // (IDX_HEADS + 1)
    topk = min(TOPK_MAX, S // 4)
    idx_scale = idx_dim ** -0.5 * IDX_HEADS ** -0.5
    att_scale = head_dim ** -0.5
    h = x
    for l in range(depth):
        mod = _adaln(c, w_mod[l], b_mod[l])
        sh1, sc1, g1, sh2, sc2, g2 = jnp.split(mod, 6, axis=-1)
        xr, gr, qlat, ctab, qi, ki, wi, ga, gb = _inproj(h, sh1, sc1, norm1_g[l], w_in[l], w_uk[l], kv_norm_g[l],
                                                        (d_rnn, d_att, kv_rank, idx_dim))
        hg = _rglru(xr, gr, conv_w[l], conv_b[l], w_rg_a[l], b_rg_a[l], w_rg_x[l], b_rg_x[l], lru_lambda[l])
        scores, params = _indexer(qi, ki, wi, topk, idx_scale)
        sel, csel = _select_gather(scores, params, ctab.reshape(B * S, kv_rank // 2), topk)
        olat = _sparse_attn(csel, sel.reshape(B * S, topk), qlat.reshape(B * S, -1), S, topk, att_scale)
        h1, u2, gates = _merge(olat, hg, ga, gb, h, g1, sh2, sc2, norm2_g[l], w_uv[l], w_att_out[l], w_rnn_out[l],
                               w_out[l], w_group[l], b_group[l], w_expert[l], b_expert[l])
        h = _moe(u2, gates, h1, g2, final_g, moe_w1[l], moe_w3[l], moe_w2[l], final_norm=(l == depth - 1))
    return h
```

```python
import functools

import jax
import jax.numpy as jnp
from jax import lax
from jax.experimental import pallas as pl
from jax.experimental.pallas import tpu as pltpu
from jax.experimental.pallas import tpu_sc as plsc

LRU_C = 8.0
CONV_W = 4
N_HEADS = 8
IDX_HEADS = 8
TOPK_MAX = 256
N_GROUPS = 4
EXP_PER_GROUP = 8
EPS = 1e-6

LANES = 128
SUBLANES = 8
SC_LANES = 16
VMEM_LIMIT = 56 * 1024 * 1024

Q_TILE = 128
KEY_TILE = 256
SC_CHUNK = 64
SC_BATCH_CHUNKS = 4
INT_MIN = -(2 ** 31)

F32 = jnp.float32
BF16 = jnp.bfloat16
I32 = jnp.int32
U32 = jnp.uint32


def _cparams(sem):
    return pltpu.CompilerParams(dimension_semantics=sem, vmem_limit_bytes=VMEM_LIMIT)


def _dot(a, b, **kw):
    return jnp.dot(a, b, preferred_element_type=F32, **kw)


def _dot_nt(a, b):
    return lax.dot_general(a, b, (((1,), (1,)), ((), ())), preferred_element_type=F32)


def _rms(xf, g):
    return xf * lax.rsqrt(jnp.mean(xf * xf, axis=-1, keepdims=True) + EPS) * g


def _mod_kernel(c_ref, w_ref, b_ref, o_ref):
    c = c_ref[...]
    o_ref[...] = _dot(jax.nn.silu(c), w_ref[...], precision=lax.Precision.HIGHEST) + b_ref[...]


def _adaln(c, w_mod, b_mod):
    B, D = c.shape
    N = w_mod.shape[1]
    tn = 1024
    return pl.pallas_call(
        _mod_kernel,
        out_shape=jax.ShapeDtypeStruct((B, N), F32),
        grid=(N // tn,),
        in_specs=[pl.BlockSpec((B, D), lambda j: (0, 0)),
                  pl.BlockSpec((D, tn), lambda j: (0, j)),
                  pl.BlockSpec((1, tn), lambda j: (0, j))],
        out_specs=pl.BlockSpec((B, tn), lambda j: (0, j)),
        compiler_params=_cparams(("arbitrary",)),
        name="adaln",
    )(c, w_mod, b_mod.reshape(1, N))


def _inproj_kernel(x_ref, sh_ref, sc_ref, g_ref, wxg_ref, wq_ref, wc_ref, wqi_ref, wki_ref, wwi_ref,
                   wmg_ref, wuk_ref, kvg_ref,
                   xr_ref, gr_ref, ql_ref, ct_ref, qi_ref, ki_ref, wi_ref, ga_ref, gb_ref, *, d_rnn, kv_rank, idx_dim):
    x = x_ref[0]
    u = _rms(x, g_ref[...]) * (1.0 + sc_ref[0]) + sh_ref[0]
    ub = u.astype(BF16)
    xg = _dot(ub, wxg_ref[...])
    xr_ref[0] = xg[:, :d_rnn].astype(BF16)
    gr_ref[0] = xg[:, d_rnn:].astype(BF16)
    q = _dot(ub, wq_ref[...]).astype(BF16)
    hd = q.shape[1] // N_HEADS
    for h in range(N_HEADS):
        ql_ref[0, :, h * kv_rank:(h + 1) * kv_rank] = _dot(q[:, h * hd:(h + 1) * hd], wuk_ref[h]).astype(BF16)
    cn = _rms(_dot(ub, wc_ref[...]), kvg_ref[...])
    half = kv_rank // 2
    lo = lax.bitcast_convert_type(cn[:, :half].astype(BF16).astype(F32), U32)
    hi = lax.bitcast_convert_type(cn[:, half:].astype(BF16).astype(F32), U32)
    ct_ref[0] = (lo >> 16) | (hi & jnp.uint32(0xFFFF0000))
    qi = _dot(ub, wqi_ref[...]).astype(BF16)
    for h in range(IDX_HEADS):
        qi_ref[0, h] = qi[:, h * idx_dim:(h + 1) * idx_dim]
    ki_ref[0] = _dot(ub, wki_ref[...])[:, :idx_dim].astype(BF16)
    wi_ref[0] = _dot_nt(wwi_ref[...], ub)
    mg = jax.nn.sigmoid(_dot(ub, wmg_ref[...]))
    dm = mg.shape[1] // 2
    ga_ref[0] = mg[:, :dm].astype(BF16)
    gb_ref[0] = mg[:, dm:].astype(BF16)


def _inproj(x, sh1, sc1, norm1_g, w_in, w_uk, kv_norm_g, dims):
    B, S, D = x.shape
    d_rnn, d_att, kv_rank, idx_dim = dims
    tm = min(512, S)
    o = 0
    wxg = w_in[:, o:o + 2 * d_rnn].astype(BF16); o += 2 * d_rnn
    wq = w_in[:, o:o + d_att].astype(BF16); o += d_att
    wc = w_in[:, o:o + kv_rank].astype(BF16); o += kv_rank
    wqi = w_in[:, o:o + IDX_HEADS * idx_dim].astype(BF16); o += IDX_HEADS * idx_dim
    wki = jnp.pad(w_in[:, o:o + idx_dim], ((0, 0), (0, LANES - idx_dim))).astype(BF16); o += idx_dim
    wwi = w_in[:, o:o + IDX_HEADS].T.astype(BF16); o += IDX_HEADS
    wmg = w_in[:, o:].astype(BF16)
    hd = d_att // N_HEADS
    wuk = jnp.transpose(w_uk, (1, 2, 0)).astype(BF16)
    const2 = lambda b, t: (0, 0)
    const3 = lambda b, t: (0, 0, 0)
    row = lambda b, t: (b, t, 0)
    per_b = lambda b, t: (b, 0, 0)
    out_shapes = (
        jax.ShapeDtypeStruct((B, S, d_rnn), BF16),
        jax.ShapeDtypeStruct((B, S, d_rnn), BF16),
        jax.ShapeDtypeStruct((B, S, N_HEADS * kv_rank), BF16),
        jax.ShapeDtypeStruct((B, S, kv_rank // 2), U32),
        jax.ShapeDtypeStruct((B, IDX_HEADS, S, idx_dim), BF16),
        jax.ShapeDtypeStruct((B, S, idx_dim), BF16),
        jax.ShapeDtypeStruct((B, IDX_HEADS, S), F32),
        jax.ShapeDtypeStruct((B, S, D), BF16),
        jax.ShapeDtypeStruct((B, S, D), BF16),
    )
    out_specs = (
        pl.BlockSpec((1, tm, d_rnn), row),
        pl.BlockSpec((1, tm, d_rnn), row),
        pl.BlockSpec((1, tm, N_HEADS * kv_rank), row),
        pl.BlockSpec((1, tm, kv_rank // 2), row),
        pl.BlockSpec((1, IDX_HEADS, tm, idx_dim), lambda b, t: (b, 0, t, 0)),
        pl.BlockSpec((1, tm, idx_dim), row),
        pl.BlockSpec((1, IDX_HEADS, tm), lambda b, t: (b, 0, t)),
        pl.BlockSpec((1, tm, D), row),
        pl.BlockSpec((1, tm, D), row),
    )
    in_specs = [
        pl.BlockSpec((1, tm, D), row),
        pl.BlockSpec((1, 1, D), per_b),
        pl.BlockSpec((1, 1, D), per_b),
        pl.BlockSpec((1, D), const2),
        pl.BlockSpec(wxg.shape, const2),
        pl.BlockSpec(wq.shape, const2),
        pl.BlockSpec(wc.shape, const2),
        pl.BlockSpec(wqi.shape, const2),
        pl.BlockSpec(wki.shape, const2),
        pl.BlockSpec(wwi.shape, const2),
        pl.BlockSpec(wmg.shape, const2),
        pl.BlockSpec(wuk.shape, const3),
        pl.BlockSpec((1, kv_rank), const2),
    ]
    return pl.pallas_call(
        functools.partial(_inproj_kernel, d_rnn=d_rnn, kv_rank=kv_rank, idx_dim=idx_dim),
        out_shape=out_shapes,
        grid=(B, S // tm),
        in_specs=in_specs,
        out_specs=out_specs,
        compiler_params=_cparams(("parallel", "parallel")),
        name="inproj",
    )(x, sh1[:, None, :], sc1[:, None, :], norm1_g.reshape(1, D), wxg, wq, wc, wqi, wki, wwi, wmg, wuk,
      kv_norm_g.reshape(1, kv_rank))


def _rglru_kernel(xr_ref, gr_ref, cw_ref, cb_ref, wa_ref, ba_ref, wx_ref, bx_ref, lam_ref, o_ref,
                  tail_ref, h_ref):
    t = pl.program_id(2)

    @pl.when(t == 0)
    def _():
        tail_ref[...] = jnp.zeros_like(tail_ref)
        h_ref[...] = jnp.zeros_like(h_ref)

    x = xr_ref[0].astype(F32)
    tt = x.shape[0]
    row = lax.broadcasted_iota(I32, x.shape, 0)
    row8 = lax.broadcasted_iota(I32, tail_ref.shape, 0)
    tail = tail_ref[...]
    cw = cw_ref[...]
    xc = x * cw[CONV_W - 1:CONV_W, :] + cb_ref[...]
    for d in range(1, CONV_W):
        rolled = pltpu.roll(x, d, 0)
        head = jnp.where(row8 < d, pltpu.roll(tail, d, 0), rolled[:SUBLANES])
        xs = jnp.concatenate([head, rolled[SUBLANES:]], axis=0)
        xc = xc + xs * cw[CONV_W - 1 - d:CONV_W - d, :]
    tail_ref[...] = x[tt - SUBLANES:]
    xb = xc.astype(BF16)
    r = jax.nn.sigmoid(_dot(xb, wa_ref[0]) + ba_ref[...])
    i = jax.nn.sigmoid(_dot(xb, wx_ref[0]) + bx_ref[...])
    log_a = (-LRU_C * jax.nn.softplus(-lam_ref[...])) * r
    a = jnp.exp(log_a)
    b = jnp.sqrt(1.0 - a * a) * (i * xc)
    d = 1
    while d < tt:
        a_s = jnp.where(row >= d, pltpu.roll(a, d, 0), 1.0)
        b_s = jnp.where(row >= d, pltpu.roll(b, d, 0), 0.0)
        b = a * b_s + b
        a = a * a_s
        d *= 2
    h = a * h_ref[...] + b
    h_ref[...] = h[tt - 1:tt, :]
    o_ref[0] = (jax.nn.gelu(gr_ref[0].astype(F32)) * h).astype(BF16)


def _rglru(xr, gr, conv_w, conv_b, w_a, b_a, w_x, b_x, lam):
    B, S, C = xr.shape
    nb, bw = w_a.shape[0], w_a.shape[1]
    tt = min(256, S)
    blk = lambda b, n, t: (b, t, n)
    vec = lambda b, n, t: (0, n)
    mat = lambda b, n, t: (n, 0, 0)
    return pl.pallas_call(
        _rglru_kernel,
        out_shape=jax.ShapeDtypeStruct((B, S, C), BF16),
        grid=(B, nb, S // tt),
        in_specs=[pl.BlockSpec((1, tt, bw), blk), pl.BlockSpec((1, tt, bw), blk),
                  pl.BlockSpec((CONV_W, bw), vec), pl.BlockSpec((1, bw), vec),
                  pl.BlockSpec((1, bw, bw), mat), pl.BlockSpec((1, bw), vec),
                  pl.BlockSpec((1, bw, bw), mat), pl.BlockSpec((1, bw), vec),
                  pl.BlockSpec((1, bw), vec)],
        out_specs=pl.BlockSpec((1, tt, bw), blk),
        scratch_shapes=[pltpu.VMEM((SUBLANES, bw), F32), pltpu.VMEM((1, bw), F32)],
        compiler_params=_cparams(("parallel", "parallel", "arbitrary")),
        name="rglru",
    )(xr, gr, conv_w, conv_b.reshape(1, C), w_a.astype(BF16), b_a.reshape(1, C), w_x.astype(BF16),
      b_x.reshape(1, C), lam.reshape(1, C))


def _sortable(x):
    b = lax.bitcast_convert_type(x, I32)
    return b ^ ((b >> 31) & jnp.int32(0x7FFFFFFF))


def _indexer_kernel(qi_ref, ki_ref, wi_ref, sc_ref, par_ref, *, topk, idx_scale):
    i = pl.program_id(1)
    n_keys = (i + 1) * Q_TILE
    n_tiles = (n_keys + KEY_TILE - 1) // KEY_TILE
    qi = qi_ref[0].reshape(IDX_HEADS * Q_TILE, qi_ref.shape[-1])
    w = wi_ref[0]
    qpos = i * Q_TILE + lax.broadcasted_iota(I32, (1, Q_TILE), 1)
    rows = lax.broadcasted_iota(I32, (KEY_TILE, Q_TILE), 0)

    def score_tile(kt, carry):
        k0 = pl.multiple_of(kt * KEY_TILE, KEY_TILE)
        rel = jnp.maximum(_dot_nt(ki_ref[0, pl.ds(k0, KEY_TILE), :], qi), 0.0)
        s = rel[:, :Q_TILE] * w[0:1, :]
        for h in range(1, IDX_HEADS):
            s = s + rel[:, h * Q_TILE:(h + 1) * Q_TILE] * w[h:h + 1, :]
        s = s * idx_scale
        s = jnp.where(rows + k0 <= qpos, s, -jnp.inf)
        sc_ref[0, 0, pl.ds(k0, KEY_TILE), :] = _sortable(s)
        return carry

    lax.fori_loop(0, n_tiles, score_tile, 0)

    def count(pred):
        def body(c, acc):
            r0 = pl.multiple_of(c * KEY_TILE, KEY_TILE)
            x = sc_ref[0, 0, pl.ds(r0, KEY_TILE), :]
            m = pred(x, r0).astype(I32)
            return acc + jnp.sum(m.reshape(KEY_TILE // SUBLANES, SUBLANES, Q_TILE), axis=0)
        acc = lax.fori_loop(0, n_tiles, body, jnp.zeros((SUBLANES, Q_TILE), I32))
        return jnp.sum(acc, axis=0, keepdims=True)

    def bit_step(j, t):
        bit = 31 - j
        cand = jnp.where(bit == 31, jnp.zeros_like(t), t + (jnp.int32(1) << jnp.minimum(bit, 30)))
        ok = count(lambda x, r0: x >= cand) >= topk
        return jnp.where(ok, cand, t)

    tau = lax.fori_loop(0, 32, bit_step, jnp.full((1, Q_TILE), INT_MIN, I32))
    n_gt = count(lambda x, r0: x > tau)
    n_eq = count(lambda x, r0: x == tau)
    need = topk - n_gt
    short = qpos + 1 <= topk
    excess = jnp.logical_and(n_eq > need, jnp.logical_not(short))
    s_total = sc_ref.shape[2]

    def cut_search(_):
        def step(j, c):
            bit = jnp.int32(1) << (jnp.int32(s_total.bit_length() - 1) - j)
            cand = c + bit
            below = count(lambda x, r0: jnp.logical_and(x == tau, rows + r0 < cand))
            return jnp.where(below < need, cand, c)
        return lax.fori_loop(0, s_total.bit_length(), step, jnp.zeros((1, Q_TILE), I32))

    any_excess = jnp.max(excess.astype(I32)) > 0
    cut = lax.cond(any_excess, cut_search, lambda _: jnp.full((1, Q_TILE), s_total, I32), 0)
    cut = jnp.where(excess, cut, s_total)
    neg_inf_key = _sortable(jnp.full((1, Q_TILE), -jnp.inf, F32))
    tau = jnp.where(short, neg_inf_key, tau)
    cut = jnp.where(short, -1, cut)
    par_ref[0, 0] = jnp.concatenate([tau, cut, jnp.zeros((SUBLANES - 2, Q_TILE), I32)], axis=0)


def _indexer(qi, ki, wi, topk, idx_scale, b0, nb):
    _, H, S, di = qi.shape
    nq = S // Q_TILE
    s_pad = ki.shape[1]
    return pl.pallas_call(
        functools.partial(_indexer_kernel, topk=topk, idx_scale=idx_scale),
        out_shape=(jax.ShapeDtypeStruct((nb, nq, s_pad, Q_TILE), I32),
                   jax.ShapeDtypeStruct((nb, nq, SUBLANES, Q_TILE), I32)),
        grid=(nb, nq),
        in_specs=[pl.BlockSpec((1, H, Q_TILE, di), lambda b, i: (b0 + b, 0, i, 0)),
                  pl.BlockSpec((1, s_pad, di), lambda b, i: (b0 + b, 0, 0)),
                  pl.BlockSpec((1, H, Q_TILE), lambda b, i: (b0 + b, 0, i))],
        out_specs=(pl.BlockSpec((1, 1, s_pad, Q_TILE), lambda b, i: (b, i, 0, 0)),
                   pl.BlockSpec((1, 1, SUBLANES, Q_TILE), lambda b, i: (b, i, 0, 0))),
        compiler_params=_cparams(("parallel", "arbitrary")),
        name="indexer",
    )(qi, ki, wi)


def _select_gather(scores, params, table, topk, seq, b0):
    nb, nq, s_pad, _ = scores.shape
    W = table.shape[1]
    info = plsc.get_sparse_core_info()
    nc, ns = info.num_cores, info.num_subcores
    nw = nc * ns
    n_units = nb * nq
    assert nw % nb == 0 and nq % (nw // nb) == 0 and topk % LANES == 0 and Q_TILE % (2 * SC_CHUNK) == 0
    stride = nw // nb
    units_per_w = n_units // nw
    n_groups = Q_TILE // SC_LANES
    idx_rows = Q_TILE * topk // LANES
    rows_per_q = topk // LANES
    mesh = plsc.VectorSubcoreMesh(core_axis_name="c", subcore_axis_name="s")

    @functools.partial(
        pl.kernel, mesh=mesh,
        out_type=(jax.ShapeDtypeStruct((n_units, idx_rows, LANES), I32),
                  jax.ShapeDtypeStruct((n_units * Q_TILE, topk, W), U32)),
        scratch_types=[
            pltpu.VMEM((2, SC_CHUNK, Q_TILE), I32),
            pltpu.VMEM((SUBLANES, Q_TILE), I32),
            pltpu.VMEM((idx_rows, LANES), I32),
            pltpu.VMEM((2, topk, W), U32),
            pltpu.SemaphoreType.DMA((2,)),
            pltpu.SemaphoreType.DMA((2,)),
            pltpu.SemaphoreType.DMA((2,)),
        ],
        compiler_params=pltpu.CompilerParams(needs_layout_passes=False),
        name="select_gather",
    )
    def k(sc_hbm, par_hbm, tab_hbm, sel_hbm, out_hbm, buf_v, par_v, idx_v, rows_v, sem_c, sem_g, sem_w):
        wid = lax.axis_index("s") * nc + lax.axis_index("c")
        lane = lax.iota(I32, SC_LANES)

        def chunk_copy(unit, c, slot):
            return pltpu.make_async_copy(sc_hbm.at[unit, pl.ds(c * SC_CHUNK, SC_CHUNK)], buf_v.at[slot],
                                         sem_c.at[slot])

        def gather_copies(q, slot):
            return [pltpu.make_async_copy(tab_hbm.at[idx_v.at[q * rows_per_q + p]],
                                          rows_v.at[slot, pl.ds(p * LANES, LANES)], sem_g.at[slot])
                    for p in range(rows_per_q)]

        def write_copy(unit, q, slot):
            return pltpu.make_async_copy(rows_v.at[slot], out_hbm.at[unit * Q_TILE + q], sem_w.at[slot])

        @pl.loop(0, units_per_w)
        def _(uu):
            b = wid % nb
            i = wid // nb + stride * uu
            unit = b * nq + i
            base = (b0 + b) * seq
            n_chunks = (i + 1) * (Q_TILE // SC_CHUNK)
            chunk_copy(unit, 0, 0).start()
            pltpu.sync_copy(par_hbm.at[unit], par_v)
            fill = jnp.zeros((SC_LANES,), I32) + base

            @pl.loop(0, idx_rows)
            def _(j):
                for g in range(LANES // SC_LANES):
                    idx_v[j, pl.ds(g * SC_LANES, SC_LANES)] = fill + ((lane + g * SC_LANES + j * LANES) & (topk - 1))

            taus = [par_v[0, pl.ds(g * SC_LANES, SC_LANES)] for g in range(n_groups)]
            cuts = [par_v[1, pl.ds(g * SC_LANES, SC_LANES)] for g in range(n_groups)]
            slot0 = [(lane + g * SC_LANES) * topk for g in range(n_groups)]

            def process(c, slot, cnts):
                def row_body(r, cnts):
                    kv = jnp.zeros((SC_LANES,), I32) + (c * SC_CHUNK + r)
                    gv = kv + base
                    ss = [buf_v[slot, r, pl.ds(g * SC_LANES, SC_LANES)] for g in range(n_groups)]
                    ms = []
                    for g in range(n_groups):
                        m = (ss[g] > taus[g]) | ((ss[g] == taus[g]) & (kv <= cuts[g]))
                        ms.append(m & (cnts[g] < topk))
                    out = []
                    for g in range(n_groups):
                        pos = slot0[g] + cnts[g]
                        plsc.store_scatter(idx_v, [pos >> 7, pos & (LANES - 1)], gv, mask=ms[g])
                        out.append(cnts[g] + ms[g].astype(I32))
                    return tuple(out)

                return plsc.parallel_loop(0, SC_CHUNK, unroll=2, carry=cnts)(row_body)

            def pair_body(p, cnts):
                c0 = 2 * p
                chunk_copy(unit, c0, 0).wait()
                chunk_copy(unit, c0 + 1, 1).start()
                cnts = process(c0, 0, cnts)
                chunk_copy(unit, c0 + 1, 1).wait()

                @pl.when(c0 + 2 < n_chunks)
                def _():
                    chunk_copy(unit, c0 + 2, 0).start()

                return process(c0 + 1, 1, cnts)

            zero = jnp.zeros((SC_LANES,), I32)
            lax.fori_loop(0, n_chunks // 2, pair_body, tuple(zero for _ in range(n_groups)))
            pltpu.sync_copy(idx_v, sel_hbm.at[unit])

            for cp in gather_copies(0, 0):
                cp.start()

            @pl.loop(0, Q_TILE // 2)
            def _(j):
                for s in range(2):
                    q = 2 * j + s
                    for cp in gather_copies(q, s):
                        cp.wait()
                    write_copy(unit, q, s).start()

                    @pl.when(q > 0)
                    def _():
                        write_copy(unit, q - 1, 1 - s).wait()

                    @pl.when(q + 1 < Q_TILE)
                    def _():
                        for cp in gather_copies(q + 1, 1 - s):
                            cp.start()

            write_copy(unit, Q_TILE - 1, 1).wait()

    sel, rows = k(scores.reshape(n_units, s_pad, Q_TILE), params.reshape(n_units, SUBLANES, Q_TILE), table)
    return sel.reshape(n_units * Q_TILE, topk), rows


def _attn_kernel(cs_ref, sel_ref, ql_ref, sl_ref, o_ref, *, seq, topk, att_scale, row0):
    tq = cs_ref.shape[0]
    t0 = row0 + pl.program_id(0) * tq
    w = cs_ref[...]
    lo = lax.bitcast_convert_type(w << 16, F32).astype(BF16)
    hi = lax.bitcast_convert_type(w & jnp.uint32(0xFFFF0000), F32).astype(BF16)
    g = jnp.concatenate([lo, hi], axis=-1)
    r = g.shape[-1]
    ql = ql_ref[...].reshape(tq, N_HEADS, r)
    logits = jnp.einsum("qhr,qkr->qhk", ql, g, preferred_element_type=F32) * att_scale
    row = t0 + lax.broadcasted_iota(I32, (tq, 1, 1), 0)
    base = (row // seq) * seq
    qpos = row - base
    sel = sel_ref[...].reshape(tq, 1, topk) - base
    logits = logits - sl_ref[...].reshape(1, N_HEADS, 1) * (qpos - sel).astype(F32)
    slot = lax.broadcasted_iota(I32, (1, 1, topk), 2)
    logits = jnp.where(slot < jnp.minimum(qpos + 1, topk), logits, -jnp.inf)
    p = jax.nn.softmax(logits, axis=-1)
    o = jnp.einsum("qhk,qkr->qhr", p.astype(BF16), g, preferred_element_type=F32)
    o_ref[...] = o.reshape(tq, N_HEADS * r).astype(BF16)


def _sparse_attn(csel, sel, qlat, seq, topk, att_scale, row0):
    T, _, wh = csel.shape
    tq = 16
    hr = qlat.shape[1]
    blk0 = row0 // tq
    slopes = (2.0 ** (-8.0 * jnp.arange(1, N_HEADS + 1, dtype=F32) / N_HEADS)).reshape(N_HEADS, 1)
    return pl.pallas_call(
        functools.partial(_attn_kernel, seq=seq, topk=topk, att_scale=att_scale, row0=row0),
        out_shape=jax.ShapeDtypeStruct((T, hr), BF16),
        grid=(T // tq,),
        in_specs=[pl.BlockSpec((tq, topk, wh), lambda i: (i, 0, 0)),
                  pl.BlockSpec((tq, topk), lambda i: (i, 0)),
                  pl.BlockSpec((tq, hr), lambda i: (blk0 + i, 0)),
                  pl.BlockSpec((N_HEADS, 1), lambda i: (0, 0))],
        out_specs=pl.BlockSpec((tq, hr), lambda i: (i, 0)),
        compiler_params=_cparams(("parallel",)),
        name="sparse_attn",
    )(csel, sel, qlat, slopes)


def _merge_kernel(ol_ref, hg_ref, ga_ref, gb_ref, x_ref, g1_ref, sh_ref, sc_ref, n2_ref, wuv_ref, wao_ref,
                  wro_ref, wo_ref, wrt_ref, brt_ref, h1_ref, u2_ref, gate_ref):
    ol = ol_ref[0]
    r = wuv_ref.shape[1]
    att = jnp.concatenate([_dot(ol[:, h * r:(h + 1) * r], wuv_ref[h]) for h in range(N_HEADS)], axis=-1)
    y_b = _dot(att.astype(BF16), wao_ref[...])
    y_a = _dot(hg_ref[0], wro_ref[...])
    mixin = ga_ref[0].astype(F32) * y_a + gb_ref[0].astype(F32) * y_b
    mix = _dot(mixin.astype(BF16), wo_ref[...])
    h1 = x_ref[0] + g1_ref[0] * mix
    h1_ref[0] = h1
    u2 = _rms(h1, n2_ref[...]) * (1.0 + sc_ref[0]) + sh_ref[0]
    u2_ref[0] = u2.astype(BF16)
    lg = _dot(u2, wrt_ref[...], precision=lax.Precision.HIGHEST) + brt_ref[...]
    n_exp = N_GROUPS * EXP_PER_GROUP
    lane = lax.broadcasted_iota(I32, lg.shape, 1)
    big = jnp.int32(LANES)
    is_g = jnp.logical_and(lane >= n_exp, lane < n_exp + N_GROUPS)
    gl = jnp.where(is_g, lg, -jnp.inf)
    gmax = jnp.max(gl, axis=-1, keepdims=True)
    g_sel = jnp.min(jnp.where(jnp.logical_and(is_g, gl == gmax), lane, big), axis=-1, keepdims=True) - n_exp
    g_w = 1.0 / jnp.sum(jnp.where(is_g, jnp.exp(gl - gmax), 0.0), axis=-1, keepdims=True)
    in_grp = jnp.logical_and(lane >= g_sel * EXP_PER_GROUP, lane < (g_sel + 1) * EXP_PER_GROUP)
    el = jnp.where(in_grp, lg, -jnp.inf)
    e1 = jnp.max(el, axis=-1, keepdims=True)
    i1 = jnp.min(jnp.where(jnp.logical_and(in_grp, el == e1), lane, big), axis=-1, keepdims=True)
    el2 = jnp.where(lane == i1, -jnp.inf, el)
    e2 = jnp.max(el2, axis=-1, keepdims=True)
    i2 = jnp.min(jnp.where(jnp.logical_and(in_grp, el2 == e2), lane, big), axis=-1, keepdims=True)
    x2 = jnp.exp(e2 - e1)
    den = 1.0 + x2
    gate_ref[0] = jnp.where(lane == i1, g_w * (1.0 / den), 0.0) + jnp.where(lane == i2, g_w * (x2 / den), 0.0)


def _merge(olat, hg, ga, gb, x, g1, sh2, sc2, norm2_g, w_uv, w_att_out, w_rnn_out, w_out, w_group, b_group,
           w_expert, b_expert):
    B, S, D = x.shape
    tm = min(512, S)
    n_exp = w_expert.shape[1]
    wuv = jnp.transpose(w_uv, (1, 0, 2)).astype(BF16)
    wrt = jnp.pad(jnp.concatenate([w_expert, w_group], axis=1), ((0, 0), (0, LANES - n_exp - N_GROUPS)))
    brt = jnp.pad(jnp.concatenate([b_expert, b_group]), (0, LANES - n_exp - N_GROUPS)).reshape(1, LANES)
    row = lambda b, t: (b, t, 0)
    per_b = lambda b, t: (b, 0, 0)
    const2 = lambda b, t: (0, 0)
    const3 = lambda b, t: (0, 0, 0)
    olat = olat.reshape(B, S, -1)
    wao, wro, wo = w_att_out.astype(BF16), w_rnn_out.astype(BF16), w_out.astype(BF16)
    return pl.pallas_call(
        _merge_kernel,
        out_shape=(jax.ShapeDtypeStruct((B, S, D), F32), jax.ShapeDtypeStruct((B, S, D), BF16),
                   jax.ShapeDtypeStruct((B, S, LANES), F32)),
        grid=(B, S // tm),
        in_specs=[pl.BlockSpec((1, tm, olat.shape[-1]), row), pl.BlockSpec((1, tm, hg.shape[-1]), row),
                  pl.BlockSpec((1, tm, D), row), pl.BlockSpec((1, tm, D), row), pl.BlockSpec((1, tm, D), row),
                  pl.BlockSpec((1, 1, D), per_b), pl.BlockSpec((1, 1, D), per_b), pl.BlockSpec((1, 1, D), per_b),
                  pl.BlockSpec((1, D), const2), pl.BlockSpec(wuv.shape, const3), pl.BlockSpec(wao.shape, const2),
                  pl.BlockSpec(wro.shape, const2), pl.BlockSpec(wo.shape, const2), pl.BlockSpec(wrt.shape, const2),
                  pl.BlockSpec((1, LANES), const2)],
        out_specs=(pl.BlockSpec((1, tm, D), row), pl.BlockSpec((1, tm, D), row), pl.BlockSpec((1, tm, LANES), row)),
        compiler_params=_cparams(("parallel", "parallel")),
        name="merge",
    )(olat, hg, ga, gb, x, g1[:, None, :], sh2[:, None, :], sc2[:, None, :], norm2_g.reshape(1, D), wuv, wao, wro,
      wo, wrt, brt)


def _moe_kernel(u_ref, gate_ref, h1_ref, g2_ref, fg_ref, w1_ref, w3_ref, w2_ref, o_ref, acc_ref, *, final_norm):
    e = pl.program_id(2)

    @pl.when(e == 0)
    def _():
        acc_ref[...] = jnp.zeros_like(acc_ref)

    u = u_ref[0]
    hid = jax.nn.silu(_dot(u, w1_ref[0])) * _dot(u, w3_ref[0])
    y = _dot(hid.astype(BF16), w2_ref[0])
    gates = gate_ref[0]
    lane = lax.broadcasted_iota(I32, gates.shape, 1)
    gcol = jnp.sum(jnp.where(lane == e, gates, 0.0), axis=-1, keepdims=True)
    acc_ref[...] += y * gcol

    @pl.when(e == pl.num_programs(2) - 1)
    def _():
        h2 = h1_ref[0] + g2_ref[0] * acc_ref[...]
        o_ref[0] = _rms(h2, fg_ref[...]) if final_norm else h2


def _moe(u2, gates, h1, g2, final_g, w1, w3, w2, final_norm):
    B, S, D = h1.shape
    n_exp, _, de = w1.shape
    tm = min(1024, S)
    row = lambda b, t, e: (b, t, 0)
    return pl.pallas_call(
        functools.partial(_moe_kernel, final_norm=final_norm),
        out_shape=jax.ShapeDtypeStruct((B, S, D), F32),
        grid=(B, S // tm, n_exp),
        in_specs=[pl.BlockSpec((1, tm, D), row), pl.BlockSpec((1, tm, LANES), row), pl.BlockSpec((1, tm, D), row),
                  pl.BlockSpec((1, 1, D), lambda b, t, e: (b, 0, 0)), pl.BlockSpec((1, D), lambda b, t, e: (0, 0)),
                  pl.BlockSpec((1, D, de), lambda b, t, e: (e, 0, 0)),
                  pl.BlockSpec((1, D, de), lambda b, t, e: (e, 0, 0)),
                  pl.BlockSpec((1, de, D), lambda b, t, e: (e, 0, 0))],
        out_specs=pl.BlockSpec((1, tm, D), row),
        scratch_shapes=[pltpu.VMEM((tm, D), F32)],
        compiler_params=_cparams(("parallel", "parallel", "arbitrary")),
        name="moe",
    )(u2, gates, h1, g2[:, None, :], final_g.reshape(1, D), w1.astype(BF16), w3.astype(BF16), w2.astype(BF16))


def kernel(x, c, w_mod, b_mod, norm1_g, w_in, conv_w, conv_b, w_rg_a, b_rg_a, w_rg_x, b_rg_x, lru_lambda, kv_norm_g, w_uk, w_uv, w_rnn_out, w_att_out, w_out, norm2_g, w_group, b_group, w_expert, b_expert, moe_w1, moe_w3, moe_w2, final_g):
    B, S, D = x.shape
    depth = w_mod.shape[0]
    d_rnn = conv_w.shape[2]
    kv_rank, _, head_dim = w_uk.shape[1:]
    d_att = N_HEADS * head_dim
    idx_dim = (w_in.shape[2] - 2 * d_rnn - d_att - kv_rank - IDX_HEADS - 2 * D) // (IDX_HEADS + 1)
    topk = min(TOPK_MAX, S // 4)
    idx_scale = idx_dim ** -0.5 * IDX_HEADS ** -0.5
    att_scale = head_dim ** -0.5
    h = x
    for l in range(depth):
        mod = _adaln(c, w_mod[l], b_mod[l])
        sh1, sc1, g1, sh2, sc2, g2 = jnp.split(mod, 6, axis=-1)
        xr, gr, qlat, ctab, qi, ki, wi, ga, gb = _inproj(h, sh1, sc1, norm1_g[l], w_in[l], w_uk[l], kv_norm_g[l],
                                                        (d_rnn, d_att, kv_rank, idx_dim))
        hg = _rglru(xr, gr, conv_w[l], conv_b[l], w_rg_a[l], b_rg_a[l], w_rg_x[l], b_rg_x[l], lru_lambda[l])
        s_pad = -(-S // KEY_TILE) * KEY_TILE
        if s_pad != S:
            ki = jnp.pad(ki, ((0, 0), (0, s_pad - S), (0, 0)))
        table = ctab.reshape(B * S, kv_rank // 2)
        qlat2 = qlat.reshape(B * S, -1)
        nch = SC_BATCH_CHUNKS if B % SC_BATCH_CHUNKS == 0 else 1
        nb = B // nch
        olats = []
        for ch in range(nch):
            b0 = ch * nb
            scores, params = _indexer(qi, ki, wi, topk, idx_scale, b0, nb)
            sel, csel = _select_gather(scores, params, table, topk, S, b0)
            olats.append(_sparse_attn(csel, sel, qlat2, S, topk, att_scale, b0 * S))
        olat = jnp.concatenate(olats, axis=0)
        h1, u2, gates = _merge(olat, hg, ga, gb, h, g1, sh2, sc2, norm2_g[l], w_uv[l], w_att_out[l], w_rnn_out[l],
                               w_out[l], w_group[l], b_group[l], w_expert[l], b_expert[l])
        h = _moe(u2, gates, h1, g2, final_g, moe_w1[l], moe_w3[l], moe_w2[l], final_norm=(l == depth - 1))
    return h
```

```python
import functools

import jax
import jax.numpy as jnp
from jax import lax
from jax.experimental import pallas as pl
from jax.experimental.pallas import tpu as pltpu
from jax.experimental.pallas import tpu_sc as plsc

LRU_C = 8.0
CONV_W = 4
N_HEADS = 8
IDX_HEADS = 8
TOPK_MAX = 256
N_GROUPS = 4
EXP_PER_GROUP = 8
EPS = 1e-6

LANES = 128
SUBLANES = 8
SC_LANES = 16
VMEM_LIMIT = 56 * 1024 * 1024

Q_TILE = 128
KEY_TILE = 256
SC_CHUNK = 64
SC_BATCH_CHUNKS = 4
SC_WIN = 64
SC_GATHER_GRANULE = 2 * 32 * SC_WIN
MOE_TILE = 512
INT_MIN = -(2 ** 31)

F32 = jnp.float32
BF16 = jnp.bfloat16
I32 = jnp.int32
U32 = jnp.uint32


def _cparams(sem):
    return pltpu.CompilerParams(dimension_semantics=sem, vmem_limit_bytes=VMEM_LIMIT)


def _dot(a, b, **kw):
    return jnp.dot(a, b, preferred_element_type=F32, **kw)


def _dot_nt(a, b):
    return lax.dot_general(a, b, (((1,), (1,)), ((), ())), preferred_element_type=F32)


def _rms(xf, g):
    return xf * lax.rsqrt(jnp.mean(xf * xf, axis=-1, keepdims=True) + EPS) * g


def _mod_kernel(c_ref, w_ref, b_ref, o_ref):
    c = c_ref[...]
    o_ref[...] = _dot(jax.nn.silu(c), w_ref[...], precision=lax.Precision.HIGHEST) + b_ref[...]


def _adaln(c, w_mod, b_mod):
    B, D = c.shape
    N = w_mod.shape[1]
    tn = 1024
    return pl.pallas_call(
        _mod_kernel,
        out_shape=jax.ShapeDtypeStruct((B, N), F32),
        grid=(N // tn,),
        in_specs=[pl.BlockSpec((B, D), lambda j: (0, 0)),
                  pl.BlockSpec((D, tn), lambda j: (0, j)),
                  pl.BlockSpec((1, tn), lambda j: (0, j))],
        out_specs=pl.BlockSpec((B, tn), lambda j: (0, j)),
        compiler_params=_cparams(("arbitrary",)),
        name="adaln",
    )(c, w_mod, b_mod.reshape(1, N))


def _inproj_kernel(x_ref, sh_ref, sc_ref, g_ref, wxg_ref, wq_ref, wc_ref, wqi_ref, wki_ref, wwi_ref,
                   wmg_ref, wuk_ref, kvg_ref,
                   xr_ref, gr_ref, ql_ref, ct_ref, qi_ref, ki_ref, wi_ref, ga_ref, gb_ref, *, d_rnn, kv_rank, idx_dim):
    x = x_ref[0]
    u = _rms(x, g_ref[...]) * (1.0 + sc_ref[0]) + sh_ref[0]
    ub = u.astype(BF16)
    xg = _dot(ub, wxg_ref[...])
    xr_ref[0] = xg[:, :d_rnn].astype(BF16)
    gr_ref[0] = xg[:, d_rnn:].astype(BF16)
    q = _dot(ub, wq_ref[...]).astype(BF16)
    hd = q.shape[1] // N_HEADS
    for h in range(N_HEADS):
        ql_ref[0, :, h * kv_rank:(h + 1) * kv_rank] = _dot(q[:, h * hd:(h + 1) * hd], wuk_ref[h]).astype(BF16)
    cn = _rms(_dot(ub, wc_ref[...]), kvg_ref[...])
    half = kv_rank // 2
    lo = lax.bitcast_convert_type(cn[:, :half].astype(BF16).astype(F32), U32)
    hi = lax.bitcast_convert_type(cn[:, half:].astype(BF16).astype(F32), U32)
    ct_ref[0] = (lo >> 16) | (hi & jnp.uint32(0xFFFF0000))
    qi = _dot(ub, wqi_ref[...]).astype(BF16)
    for h in range(IDX_HEADS):
        qi_ref[0, h] = qi[:, h * idx_dim:(h + 1) * idx_dim]
    ki_ref[0] = _dot(ub, wki_ref[...])[:, :idx_dim].astype(BF16)
    wi_ref[0] = _dot_nt(wwi_ref[...], ub)
    mg = jax.nn.sigmoid(_dot(ub, wmg_ref[...]))
    dm = mg.shape[1] // 2
    ga_ref[0] = mg[:, :dm].astype(BF16)
    gb_ref[0] = mg[:, dm:].astype(BF16)


def _inproj(x, sh1, sc1, norm1_g, w_in, w_uk, kv_norm_g, dims):
    B, S, D = x.shape
    d_rnn, d_att, kv_rank, idx_dim = dims
    tm = min(512, S)
    o = 0
    wxg = w_in[:, o:o + 2 * d_rnn].astype(BF16); o += 2 * d_rnn
    wq = w_in[:, o:o + d_att].astype(BF16); o += d_att
    wc = w_in[:, o:o + kv_rank].astype(BF16); o += kv_rank
    wqi = w_in[:, o:o + IDX_HEADS * idx_dim].astype(BF16); o += IDX_HEADS * idx_dim
    wki = jnp.pad(w_in[:, o:o + idx_dim], ((0, 0), (0, LANES - idx_dim))).astype(BF16); o += idx_dim
    wwi = w_in[:, o:o + IDX_HEADS].T.astype(BF16); o += IDX_HEADS
    wmg = w_in[:, o:].astype(BF16)
    hd = d_att // N_HEADS
    wuk = jnp.transpose(w_uk, (1, 2, 0)).astype(BF16)
    const2 = lambda b, t: (0, 0)
    const3 = lambda b, t: (0, 0, 0)
    row = lambda b, t: (b, t, 0)
    per_b = lambda b, t: (b, 0, 0)
    out_shapes = (
        jax.ShapeDtypeStruct((B, S, d_rnn), BF16),
        jax.ShapeDtypeStruct((B, S, d_rnn), BF16),
        jax.ShapeDtypeStruct((B, S, N_HEADS * kv_rank), BF16),
        jax.ShapeDtypeStruct((B, S, kv_rank // 2), U32),
        jax.ShapeDtypeStruct((B, IDX_HEADS, S, idx_dim), BF16),
        jax.ShapeDtypeStruct((B, S, idx_dim), BF16),
        jax.ShapeDtypeStruct((B, IDX_HEADS, S), F32),
        jax.ShapeDtypeStruct((B, S, D), BF16),
        jax.ShapeDtypeStruct((B, S, D), BF16),
    )
    out_specs = (
        pl.BlockSpec((1, tm, d_rnn), row),
        pl.BlockSpec((1, tm, d_rnn), row),
        pl.BlockSpec((1, tm, N_HEADS * kv_rank), row),
        pl.BlockSpec((1, tm, kv_rank // 2), row),
        pl.BlockSpec((1, IDX_HEADS, tm, idx_dim), lambda b, t: (b, 0, t, 0)),
        pl.BlockSpec((1, tm, idx_dim), row),
        pl.BlockSpec((1, IDX_HEADS, tm), lambda b, t: (b, 0, t)),
        pl.BlockSpec((1, tm, D), row),
        pl.BlockSpec((1, tm, D), row),
    )
    in_specs = [
        pl.BlockSpec((1, tm, D), row),
        pl.BlockSpec((1, 1, D), per_b),
        pl.BlockSpec((1, 1, D), per_b),
        pl.BlockSpec((1, D), const2),
        pl.BlockSpec(wxg.shape, const2),
        pl.BlockSpec(wq.shape, const2),
        pl.BlockSpec(wc.shape, const2),
        pl.BlockSpec(wqi.shape, const2),
        pl.BlockSpec(wki.shape, const2),
        pl.BlockSpec(wwi.shape, const2),
        pl.BlockSpec(wmg.shape, const2),
        pl.BlockSpec(wuk.shape, const3),
        pl.BlockSpec((1, kv_rank), const2),
    ]
    return pl.pallas_call(
        functools.partial(_inproj_kernel, d_rnn=d_rnn, kv_rank=kv_rank, idx_dim=idx_dim),
        out_shape=out_shapes,
        grid=(B, S // tm),
        in_specs=in_specs,
        out_specs=out_specs,
        compiler_params=_cparams(("parallel", "parallel")),
        name="inproj",
    )(x, sh1[:, None, :], sc1[:, None, :], norm1_g.reshape(1, D), wxg, wq, wc, wqi, wki, wwi, wmg, wuk,
      kv_norm_g.reshape(1, kv_rank))


def _rglru_kernel(xr_ref, gr_ref, cw_ref, cb_ref, wa_ref, ba_ref, wx_ref, bx_ref, lam_ref, o_ref, *, tt):
    seq, c = xr_ref.shape[1], xr_ref.shape[2]
    row = lax.broadcasted_iota(I32, (tt, c), 0)
    row8 = lax.broadcasted_iota(I32, (SUBLANES, c), 0)
    cw = cw_ref[...]
    cb = cb_ref[...]
    wa, wx = wa_ref[0], wx_ref[0]
    ba, bx = ba_ref[...], bx_ref[...]
    decay = -LRU_C * jax.nn.softplus(-lam_ref[...])

    def tile(t, carry):
        tail, h_prev = carry
        t0 = pl.multiple_of(t * tt, tt)
        x = xr_ref[0, pl.ds(t0, tt), :].astype(F32)
        xc = x * cw[CONV_W - 1:CONV_W, :] + cb
        for d in range(1, CONV_W):
            rolled = pltpu.roll(x, d, 0)
            head = jnp.where(row8 < d, pltpu.roll(tail, d, 0), rolled[:SUBLANES])
            xs = jnp.concatenate([head, rolled[SUBLANES:]], axis=0)
            xc = xc + xs * cw[CONV_W - 1 - d:CONV_W - d, :]
        xb = xc.astype(BF16)
        r = jax.nn.sigmoid(_dot(xb, wa) + ba)
        i = jax.nn.sigmoid(_dot(xb, wx) + bx)
        log_a = decay * r
        a = jnp.exp(log_a)
        b = jnp.sqrt(1.0 - a * a) * (i * xc)
        d = 1
        while d < tt:
            a_s = jnp.where(row >= d, pltpu.roll(a, d, 0), 1.0)
            b_s = jnp.where(row >= d, pltpu.roll(b, d, 0), 0.0)
            b = a * b_s + b
            a = a * a_s
            d *= 2
        h = a * h_prev + b
        o_ref[0, pl.ds(t0, tt), :] = (jax.nn.gelu(gr_ref[0, pl.ds(t0, tt), :].astype(F32)) * h).astype(BF16)
        return x[tt - SUBLANES:], h[tt - 1:tt, :]

    lax.fori_loop(0, seq // tt, tile, (jnp.zeros((SUBLANES, c), F32), jnp.zeros((1, c), F32)))


def _rglru(xr, gr, conv_w, conv_b, w_a, b_a, w_x, b_x, lam):
    B, S, C = xr.shape
    nb, bw = w_a.shape[0], w_a.shape[1]
    tt = min(256, S)
    blk = lambda b, n: (b, 0, n)
    vec = lambda b, n: (0, n)
    mat = lambda b, n: (n, 0, 0)
    return pl.pallas_call(
        functools.partial(_rglru_kernel, tt=tt),
        out_shape=jax.ShapeDtypeStruct((B, S, C), BF16),
        grid=(B, nb),
        in_specs=[pl.BlockSpec((1, S, bw), blk), pl.BlockSpec((1, S, bw), blk),
                  pl.BlockSpec((CONV_W, bw), vec), pl.BlockSpec((1, bw), vec),
                  pl.BlockSpec((1, bw, bw), mat), pl.BlockSpec((1, bw), vec),
                  pl.BlockSpec((1, bw, bw), mat), pl.BlockSpec((1, bw), vec),
                  pl.BlockSpec((1, bw), vec)],
        out_specs=pl.BlockSpec((1, S, bw), blk),
        compiler_params=_cparams(("parallel", "parallel")),
        name="rglru",
    )(xr, gr, conv_w, conv_b.reshape(1, C), w_a.astype(BF16), b_a.reshape(1, C), w_x.astype(BF16),
      b_x.reshape(1, C), lam.reshape(1, C))


def _sortable(x):
    b = lax.bitcast_convert_type(x, I32)
    return b ^ ((b >> 31) & jnp.int32(0x7FFFFFFF))


def _indexer_kernel(qi_ref, ki_ref, wi_ref, sc_ref, par_ref, *, topk, idx_scale):
    i = pl.program_id(1)
    n_keys = (i + 1) * Q_TILE
    n_tiles = (n_keys + KEY_TILE - 1) // KEY_TILE
    qi = qi_ref[0].reshape(IDX_HEADS * Q_TILE, qi_ref.shape[-1])
    w = wi_ref[0]
    qpos = i * Q_TILE + lax.broadcasted_iota(I32, (1, Q_TILE), 1)
    rows = lax.broadcasted_iota(I32, (KEY_TILE, Q_TILE), 0)

    def score_tile(kt, carry):
        k0 = pl.multiple_of(kt * KEY_TILE, KEY_TILE)
        rel = jnp.maximum(_dot_nt(ki_ref[0, pl.ds(k0, KEY_TILE), :], qi), 0.0)
        s = rel[:, :Q_TILE] * w[0:1, :]
        for h in range(1, IDX_HEADS):
            s = s + rel[:, h * Q_TILE:(h + 1) * Q_TILE] * w[h:h + 1, :]
        s = s * idx_scale
        s = jnp.where(rows + k0 <= qpos, s, -jnp.inf)
        sc_ref[0, 0, pl.ds(k0, KEY_TILE), :] = _sortable(s)
        return carry

    lax.fori_loop(0, n_tiles, score_tile, 0)

    def count(pred):
        def body(c, acc):
            r0 = pl.multiple_of(c * KEY_TILE, KEY_TILE)
            x = sc_ref[0, 0, pl.ds(r0, KEY_TILE), :]
            m = pred(x, r0).astype(I32)
            return acc + jnp.sum(m.reshape(KEY_TILE // SUBLANES, SUBLANES, Q_TILE), axis=0)
        acc = lax.fori_loop(0, n_tiles, body, jnp.zeros((SUBLANES, Q_TILE), I32))
        return jnp.sum(acc, axis=0, keepdims=True)

    short = qpos + 1 <= topk
    s_total = sc_ref.shape[2]

    def searching(state):
        j, _, n_t = state
        settled = jnp.logical_or(n_t == topk, short)
        return jnp.logical_and(j < 32, jnp.min(settled.astype(I32)) == 0)

    def bit_step(state):
        j, t, n_t = state
        bit = 31 - j
        cand = jnp.where(bit == 31, jnp.zeros_like(t), t + (jnp.int32(1) << jnp.minimum(bit, 30)))
        n_c = count(lambda x, r0: x >= cand)
        ok = n_c >= topk
        return j + 1, jnp.where(ok, cand, t), jnp.where(ok, n_c, n_t)

    t_init = jnp.full((1, Q_TILE), INT_MIN, I32)
    _, t_ge, n_ge = lax.while_loop(searching, bit_step, (jnp.int32(0), t_init, jnp.full((1, Q_TILE), s_total, I32)))
    tied = jnp.logical_and(n_ge > topk, jnp.logical_not(short))

    def tie_cut(_):
        need = topk - count(lambda x, r0: x > t_ge)

        def step(j, c):
            bit = jnp.int32(1) << (jnp.int32(s_total.bit_length() - 1) - j)
            cand = c + bit
            below = count(lambda x, r0: jnp.logical_and(x == t_ge, rows + r0 < cand))
            return jnp.where(below < need, cand, c)
        return lax.fori_loop(0, s_total.bit_length(), step, jnp.zeros((1, Q_TILE), I32))

    any_tied = jnp.max(tied.astype(I32)) > 0
    cut = lax.cond(any_tied, tie_cut, lambda _: jnp.zeros((1, Q_TILE), I32), 0)
    tau = jnp.where(tied, t_ge, t_ge - 1)
    cut = jnp.where(tied, cut, -1)
    neg_inf_key = _sortable(jnp.full((1, Q_TILE), -jnp.inf, F32))
    tau = jnp.where(short, neg_inf_key, tau)
    cut = jnp.where(short, -1, cut)
    par_ref[0, 0] = jnp.concatenate([tau, cut, jnp.zeros((SUBLANES - 2, Q_TILE), I32)], axis=0)


def _indexer(qi, ki, wi, topk, idx_scale, b0, nb):
    _, H, S, di = qi.shape
    nq = S // Q_TILE
    s_pad = ki.shape[1]
    return pl.pallas_call(
        functools.partial(_indexer_kernel, topk=topk, idx_scale=idx_scale),
        out_shape=(jax.ShapeDtypeStruct((nb, nq, s_pad, Q_TILE), I32),
                   jax.ShapeDtypeStruct((nb, nq, SUBLANES, Q_TILE), I32)),
        grid=(nb, nq),
        in_specs=[pl.BlockSpec((1, H, Q_TILE, di), lambda b, i: (b0 + b, 0, i, 0)),
                  pl.BlockSpec((1, s_pad, di), lambda b, i: (b0 + b, 0, 0)),
                  pl.BlockSpec((1, H, Q_TILE), lambda b, i: (b0 + b, 0, i))],
        out_specs=(pl.BlockSpec((1, 1, s_pad, Q_TILE), lambda b, i: (b, i, 0, 0)),
                   pl.BlockSpec((1, 1, SUBLANES, Q_TILE), lambda b, i: (b, i, 0, 0))),
        compiler_params=_cparams(("parallel", "arbitrary")),
        name="indexer",
    )(qi, ki, wi)


def _select_gather(scores, params, table, topk, seq, b0):
    nb, nq, s_pad, _ = scores.shape
    W = table.shape[1]
    info = plsc.get_sparse_core_info()
    nc, ns = info.num_cores, info.num_subcores
    nw = nc * ns
    n_units = nb * nq
    assert nw % nb == 0 and nq % (nw // nb) == 0 and topk % LANES == 0 and Q_TILE % (2 * SC_CHUNK) == 0
    stride = nw // nb
    units_per_w = n_units // nw
    n_groups = Q_TILE // SC_LANES
    idx_rows = Q_TILE * topk // LANES
    rows_per_q = topk // LANES
    mesh = plsc.VectorSubcoreMesh(core_axis_name="c", subcore_axis_name="s")

    @functools.partial(
        pl.kernel, mesh=mesh,
        out_type=(jax.ShapeDtypeStruct((n_units, idx_rows, LANES), I32),
                  jax.ShapeDtypeStruct((n_units * Q_TILE, topk, W), U32)),
        scratch_types=[
            pltpu.VMEM((2, SC_CHUNK, Q_TILE), I32),
            pltpu.VMEM((SUBLANES, Q_TILE), I32),
            pltpu.VMEM((idx_rows, LANES), I32),
            pltpu.VMEM((2, topk, W), U32),
            pltpu.SemaphoreType.DMA((2,)),
            pltpu.SemaphoreType.DMA((2,)),
            pltpu.SemaphoreType.DMA((2,)),
        ],
        compiler_params=pltpu.CompilerParams(needs_layout_passes=False),
        name="select_gather",
    )
    def k(sc_hbm, par_hbm, tab_hbm, sel_hbm, out_hbm, buf_v, par_v, idx_v, rows_v, sem_c, sem_g, sem_w):
        wid = lax.axis_index("s") * nc + lax.axis_index("c")
        lane = lax.iota(I32, SC_LANES)

        def chunk_copy(unit, c, slot):
            return pltpu.make_async_copy(sc_hbm.at[unit, pl.ds(c * SC_CHUNK, SC_CHUNK)], buf_v.at[slot],
                                         sem_c.at[slot])

        def gather_copies(q, slot):
            return [pltpu.make_async_copy(tab_hbm.at[idx_v.at[q * rows_per_q + p]],
                                          rows_v.at[slot, pl.ds(p * LANES, LANES)], sem_g.at[slot])
                    for p in range(rows_per_q)]

        def write_copy(unit, q, slot):
            return pltpu.make_async_copy(rows_v.at[slot], out_hbm.at[unit * Q_TILE + q], sem_w.at[slot])

        @pl.loop(0, units_per_w)
        def _(uu):
            b = wid % nb
            i = wid // nb + stride * uu
            unit = b * nq + i
            base = (b0 + b) * seq
            n_chunks = (i + 1) * (Q_TILE // SC_CHUNK)
            chunk_copy(unit, 0, 0).start()
            pltpu.sync_copy(par_hbm.at[unit], par_v)
            fill = jnp.zeros((SC_LANES,), I32) + base

            @pl.loop(0, idx_rows)
            def _(j):
                for g in range(LANES // SC_LANES):
                    idx_v[j, pl.ds(g * SC_LANES, SC_LANES)] = fill + ((lane + g * SC_LANES + j * LANES) & (topk - 1))

            taus = [par_v[0, pl.ds(g * SC_LANES, SC_LANES)] for g in range(n_groups)]
            cuts = [par_v[1, pl.ds(g * SC_LANES, SC_LANES)] for g in range(n_groups)]
            slot0 = [(lane + g * SC_LANES) * topk for g in range(n_groups)]

            def process(c, slot, cnts):
                def row_body(r, cnts):
                    kv = jnp.zeros((SC_LANES,), I32) + (c * SC_CHUNK + r)
                    gv = kv + base
                    ss = [buf_v[slot, r, pl.ds(g * SC_LANES, SC_LANES)] for g in range(n_groups)]
                    ms = []
                    for g in range(n_groups):
                        m = (ss[g] > taus[g]) | ((ss[g] == taus[g]) & (kv <= cuts[g]))
                        ms.append(m & (cnts[g] < topk))
                    out = []
                    for g in range(n_groups):
                        pos = slot0[g] + cnts[g]
                        plsc.store_scatter(idx_v, [pos >> 7, pos & (LANES - 1)], gv, mask=ms[g])
                        out.append(cnts[g] + ms[g].astype(I32))
                    return tuple(out)

                return plsc.parallel_loop(0, SC_CHUNK, unroll=2, carry=cnts)(row_body)

            def pair_body(p, cnts):
                c0 = 2 * p
                chunk_copy(unit, c0, 0).wait()
                chunk_copy(unit, c0 + 1, 1).start()
                cnts = process(c0, 0, cnts)
                chunk_copy(unit, c0 + 1, 1).wait()

                @pl.when(c0 + 2 < n_chunks)
                def _():
                    chunk_copy(unit, c0 + 2, 0).start()

                return process(c0 + 1, 1, cnts)

            zero = jnp.zeros((SC_LANES,), I32)
            lax.fori_loop(0, n_chunks // 2, pair_body, tuple(zero for _ in range(n_groups)))
            pltpu.sync_copy(idx_v, sel_hbm.at[unit])

            for cp in gather_copies(0, 0):
                cp.start()

            @pl.loop(0, Q_TILE // 2)
            def _(j):
                for s in range(2):
                    q = 2 * j + s
                    for cp in gather_copies(q, s):
                        cp.wait()
                    write_copy(unit, q, s).start()

                    @pl.when(q > 0)
                    def _():
                        write_copy(unit, q - 1, 1 - s).wait()

                    @pl.when(q + 1 < Q_TILE)
                    def _():
                        for cp in gather_copies(q + 1, 1 - s):
                            cp.start()

            write_copy(unit, Q_TILE - 1, 1).wait()

    sel, rows = k(scores.reshape(n_units, s_pad, Q_TILE), params.reshape(n_units, SUBLANES, Q_TILE), table)
    return sel.reshape(n_units * Q_TILE, topk), rows


def _attn_kernel(cs_ref, sel_ref, ql_ref, sl_ref, o_ref, *, seq, topk, att_scale, row0):
    tq = cs_ref.shape[0]
    t0 = row0 + pl.program_id(0) * tq
    w = cs_ref[...]
    lo = lax.bitcast_convert_type(w << 16, F32).astype(BF16)
    hi = lax.bitcast_convert_type(w & jnp.uint32(0xFFFF0000), F32).astype(BF16)
    g = jnp.concatenate([lo, hi], axis=-1)
    r = g.shape[-1]
    ql = ql_ref[...].reshape(tq, N_HEADS, r)
    logits = jnp.einsum("qhr,qkr->qhk", ql, g, preferred_element_type=F32) * att_scale
    row = t0 + lax.broadcasted_iota(I32, (tq, 1, 1), 0)
    base = (row // seq) * seq
    qpos = row - base
    sel = sel_ref[...].reshape(tq, 1, topk) - base
    logits = logits - sl_ref[...].reshape(1, N_HEADS, 1) * (qpos - sel).astype(F32)
    slot = lax.broadcasted_iota(I32, (1, 1, topk), 2)
    logits = jnp.where(slot < jnp.minimum(qpos + 1, topk), logits, -jnp.inf)
    p = jax.nn.softmax(logits, axis=-1)
    o = jnp.einsum("qhk,qkr->qhr", p.astype(BF16), g, preferred_element_type=F32)
    o_ref[...] = o.reshape(tq, N_HEADS * r).astype(BF16)


def _attn_kernel_inplace(cs_ref, sel_ref, ql_ref, sl_ref, prev_ref, o_ref, **kw):
    del prev_ref
    _attn_kernel(cs_ref, sel_ref, ql_ref, sl_ref, o_ref, **kw)


def _sparse_attn(csel, sel, qlat, seq, topk, att_scale, row0, olat_prev):
    T, _, wh = csel.shape
    tq = 16
    hr = qlat.shape[1]
    blk0 = row0 // tq
    slopes = (2.0 ** (-8.0 * jnp.arange(1, N_HEADS + 1, dtype=F32) / N_HEADS)).reshape(N_HEADS, 1)
    kw = dict(seq=seq, topk=topk, att_scale=att_scale, row0=row0)
    in_specs = [pl.BlockSpec((tq, topk, wh), lambda i: (i, 0, 0)),
                pl.BlockSpec((tq, topk), lambda i: (i, 0)),
                pl.BlockSpec((tq, hr), lambda i: (blk0 + i, 0)),
                pl.BlockSpec((N_HEADS, 1), lambda i: (0, 0))]
    args = [csel, sel, qlat, slopes]
    if olat_prev is None:
        body, aliases = functools.partial(_attn_kernel, **kw), {}
    else:
        body, aliases = functools.partial(_attn_kernel_inplace, **kw), {len(args): 0}
        in_specs.append(pl.BlockSpec(memory_space=pl.ANY))
        args.append(olat_prev)
    return pl.pallas_call(
        body,
        out_shape=jax.ShapeDtypeStruct(qlat.shape, BF16),
        grid=(T // tq,),
        in_specs=in_specs,
        out_specs=pl.BlockSpec((tq, hr), lambda i: (blk0 + i, 0)),
        input_output_aliases=aliases,
        compiler_params=_cparams(("parallel",)),
        name="sparse_attn",
    )(*args)


def _pack_halves(x):
    n = x.shape[1] // 2
    xb = x.astype(BF16).astype(F32)
    return (lax.bitcast_convert_type(xb[:, :n], U32) >> 16) | (lax.bitcast_convert_type(xb[:, n:], U32)
                                                              & jnp.uint32(0xFFFF0000))


def _unpack_lo(w):
    return lax.bitcast_convert_type(w << 16, F32)


def _unpack_hi(w):
    return lax.bitcast_convert_type(w & jnp.uint32(0xFFFF0000), F32)


def _merge_kernel(ol_ref, hg_ref, ga_ref, gb_ref, x_ref, g1_ref, sh_ref, sc_ref, n2_ref, wuv_ref, wao_ref,
                  wro_ref, wo_ref, wrt_ref, brt_ref, h1_ref, xrow_ref, route_ref):
    ol = ol_ref[0]
    r = wuv_ref.shape[1]
    att = jnp.concatenate([_dot(ol[:, h * r:(h + 1) * r], wuv_ref[h]) for h in range(N_HEADS)], axis=-1)
    y_b = _dot(att.astype(BF16), wao_ref[...])
    y_a = _dot(hg_ref[0], wro_ref[...])
    mixin = ga_ref[0].astype(F32) * y_a + gb_ref[0].astype(F32) * y_b
    mix = _dot(mixin.astype(BF16), wo_ref[...])
    h1 = x_ref[0] + g1_ref[0] * mix
    h1_ref[0] = h1
    u2 = _rms(h1, n2_ref[...]) * (1.0 + sc_ref[0]) + sh_ref[0]
    lg = _dot(u2, wrt_ref[...], precision=lax.Precision.HIGHEST) + brt_ref[...]
    n_exp = N_GROUPS * EXP_PER_GROUP
    lane = lax.broadcasted_iota(I32, lg.shape, 1)
    big = jnp.int32(LANES)
    is_g = jnp.logical_and(lane >= n_exp, lane < n_exp + N_GROUPS)
    gl = jnp.where(is_g, lg, -jnp.inf)
    gmax = jnp.max(gl, axis=-1, keepdims=True)
    g_sel = jnp.min(jnp.where(jnp.logical_and(is_g, gl == gmax), lane, big), axis=-1, keepdims=True) - n_exp
    g_w = 1.0 / jnp.sum(jnp.where(is_g, jnp.exp(gl - gmax), 0.0), axis=-1, keepdims=True)
    in_grp = jnp.logical_and(lane >= g_sel * EXP_PER_GROUP, lane < (g_sel + 1) * EXP_PER_GROUP)
    el = jnp.where(in_grp, lg, -jnp.inf)
    e1 = jnp.max(el, axis=-1, keepdims=True)
    i1 = jnp.min(jnp.where(jnp.logical_and(in_grp, el == e1), lane, big), axis=-1, keepdims=True)
    el2 = jnp.where(lane == i1, -jnp.inf, el)
    e2 = jnp.max(el2, axis=-1, keepdims=True)
    i2 = jnp.min(jnp.where(jnp.logical_and(in_grp, el2 == e2), lane, big), axis=-1, keepdims=True)
    x2 = jnp.exp(e2 - e1)
    den = 1.0 + x2
    gates = jnp.where(lane == i1, g_w * (1.0 / den), 0.0) + jnp.where(lane == i2, g_w * (x2 / den), 0.0)
    gate_words = lax.bitcast_convert_type(gates.astype(BF16).astype(F32), U32) >> 16
    xrow_ref[0] = jnp.concatenate([_pack_halves(u2), gate_words], axis=-1)
    route_ref[0] = jnp.where(lane == 0, i1, jnp.where(lane == 1, i2, 0))


def _merge(olat, hg, ga, gb, x, g1, sh2, sc2, norm2_g, w_uv, w_att_out, w_rnn_out, w_out, w_group, b_group,
           w_expert, b_expert):
    B, S, D = x.shape
    tm = min(512, S)
    n_exp = w_expert.shape[1]
    wuv = jnp.transpose(w_uv, (1, 0, 2)).astype(BF16)
    wrt = jnp.pad(jnp.concatenate([w_expert, w_group], axis=1), ((0, 0), (0, LANES - n_exp - N_GROUPS)))
    brt = jnp.pad(jnp.concatenate([b_expert, b_group]), (0, LANES - n_exp - N_GROUPS)).reshape(1, LANES)
    row = lambda b, t: (b, t, 0)
    per_b = lambda b, t: (b, 0, 0)
    const2 = lambda b, t: (0, 0)
    const3 = lambda b, t: (0, 0, 0)
    olat = olat.reshape(B, S, -1)
    wao, wro, wo = w_att_out.astype(BF16), w_rnn_out.astype(BF16), w_out.astype(BF16)
    return pl.pallas_call(
        _merge_kernel,
        out_shape=(jax.ShapeDtypeStruct((B, S, D), F32), jax.ShapeDtypeStruct((B, S, D // 2 + LANES), U32),
                   jax.ShapeDtypeStruct((B, S, LANES), I32)),
        grid=(B, S // tm),
        in_specs=[pl.BlockSpec((1, tm, olat.shape[-1]), row), pl.BlockSpec((1, tm, hg.shape[-1]), row),
                  pl.BlockSpec((1, tm, D), row), pl.BlockSpec((1, tm, D), row), pl.BlockSpec((1, tm, D), row),
                  pl.BlockSpec((1, 1, D), per_b), pl.BlockSpec((1, 1, D), per_b), pl.BlockSpec((1, 1, D), per_b),
                  pl.BlockSpec((1, D), const2), pl.BlockSpec(wuv.shape, const3), pl.BlockSpec(wao.shape, const2),
                  pl.BlockSpec(wro.shape, const2), pl.BlockSpec(wo.shape, const2), pl.BlockSpec(wrt.shape, const2),
                  pl.BlockSpec((1, LANES), const2)],
        out_specs=(pl.BlockSpec((1, tm, D), row), pl.BlockSpec((1, tm, D // 2 + LANES), row),
                   pl.BlockSpec((1, tm, LANES), row)),
        compiler_params=_cparams(("parallel", "parallel")),
        name="merge",
    )(olat, hg, ga, gb, x, g1[:, None, :], sh2[:, None, :], sc2[:, None, :], norm2_g.reshape(1, D), wuv, wao, wro,
      wo, wrt, brt)


def _sc_row_gather(table, idx):
    _, W = table.shape
    M = idx.shape[0]
    info = plsc.get_sparse_core_info()
    nc, ns = info.num_cores, info.num_subcores
    nw = nc * ns
    assert M % (nw * SC_WIN * 2) == 0
    n_win = M // (nw * SC_WIN)
    mesh = plsc.VectorSubcoreMesh(core_axis_name="c", subcore_axis_name="s")

    @functools.partial(
        pl.kernel, mesh=mesh,
        out_type=jax.ShapeDtypeStruct((M, W), U32),
        scratch_types=[pltpu.VMEM((2, SC_WIN), I32), pltpu.VMEM((2, SC_WIN, W), U32),
                       pltpu.SemaphoreType.DMA((2,)), pltpu.SemaphoreType.DMA((2,))],
        compiler_params=pltpu.CompilerParams(needs_layout_passes=False),
        name="row_gather",
    )
    def k(tab_hbm, idx_hbm, out_hbm, idx_v, rows_v, sem_g, sem_w):
        wid = lax.axis_index("s") * nc + lax.axis_index("c")

        def win_id(j):
            return j * nw + wid

        def gather(slot):
            return pltpu.make_async_copy(tab_hbm.at[idx_v.at[slot]], rows_v.at[slot], sem_g.at[slot])

        def write(j, slot):
            return pltpu.make_async_copy(rows_v.at[slot], out_hbm.at[pl.ds(win_id(j) * SC_WIN, SC_WIN)],
                                         sem_w.at[slot])

        pltpu.sync_copy(idx_hbm.at[win_id(0)], idx_v.at[0])
        gather(0).start()

        @pl.loop(0, n_win // 2)
        def _(jj):
            for s in range(2):
                j = 2 * jj + s
                gather(s).wait()
                write(j, s).start()

                @pl.when(j > 0)
                def _():
                    write(j - 1, 1 - s).wait()

                @pl.when(j + 1 < n_win)
                def _():
                    pltpu.sync_copy(idx_hbm.at[win_id(j + 1)], idx_v.at[1 - s])
                    gather(1 - s).start()

        write(n_win - 1, 1).wait()

    return k(table, idx.reshape(M // SC_WIN, SC_WIN))


def _moe_dispatch(e12, n_exp, tmm, granule):
    T = e12.shape[0]
    A = 2 * T
    a_pad = -(-(A + n_exp * tmm) // granule) * granule
    n_tiles = a_pad // tmm
    eid = e12.reshape(A)
    onehot = (eid[:, None] == jnp.arange(n_exp, dtype=I32)[None, :]).astype(I32)
    csum = jnp.cumsum(onehot, axis=0)
    rank = jnp.sum(onehot * csum, axis=1) - 1
    counts = csum[-1]
    padded = (counts + tmm - 1) // tmm * tmm
    pend = jnp.cumsum(padded)
    dest = (pend - padded)[eid] + rank
    src = (jnp.arange(a_pad, dtype=I32) % T).at[dest].set(jnp.arange(A, dtype=I32) // 2)
    tile_expert = jnp.minimum(jnp.searchsorted(pend, jnp.arange(n_tiles, dtype=I32) * tmm, side="right"),
                              n_exp - 1).astype(I32)
    n_valid = (pend[-1] // tmm).astype(I32).reshape(1)
    return src, dest.reshape(T, 2).astype(I32), tile_expert, n_valid


def _moe_grouped_kernel(te_ref, nv_ref, xs_ref, w1_ref, w3_ref, w2_ref, y_ref):
    i = pl.program_id(0)

    @pl.when(i < nv_ref[0])
    def _():
        w = xs_ref[...]
        half = y_ref.shape[1]
        wu = w[:, :half]
        u = jnp.concatenate([_unpack_lo(wu).astype(BF16), _unpack_hi(wu).astype(BF16)], axis=-1)
        gates = _unpack_lo(w[:, half:])
        lane = lax.broadcasted_iota(I32, gates.shape, 1)
        gcol = jnp.sum(jnp.where(lane == te_ref[i], gates, 0.0), axis=-1, keepdims=True)
        hid = jax.nn.silu(_dot(u, w1_ref[0])) * _dot(u, w3_ref[0])
        y_ref[...] = _pack_halves(_dot(hid.astype(BF16), w2_ref[0]) * gcol)


def _moe_grouped(xs, tile_expert, n_valid, w1, w3, w2, tmm):
    a_pad, W = xs.shape
    n_exp, D, de = w1.shape
    wmap = lambda i, te, nv: (te[i], 0, 0)
    return pl.pallas_call(
        _moe_grouped_kernel,
        out_shape=jax.ShapeDtypeStruct((a_pad, D // 2), U32),
        grid_spec=pltpu.PrefetchScalarGridSpec(
            num_scalar_prefetch=2, grid=(a_pad // tmm,),
            in_specs=[pl.BlockSpec((tmm, W), lambda i, te, nv: (i, 0)),
                      pl.BlockSpec((1, D, de), wmap), pl.BlockSpec((1, D, de), wmap), pl.BlockSpec((1, de, D), wmap)],
            out_specs=pl.BlockSpec((tmm, D // 2), lambda i, te, nv: (i, 0))),
        compiler_params=_cparams(("arbitrary",)),
        name="moe_grouped",
    )(tile_expert, n_valid, xs, w1.astype(BF16), w3.astype(BF16), w2.astype(BF16))


def _final_kernel(h1_ref, y0_ref, y1_ref, g2_ref, fg_ref, o_ref, *, final_norm):
    w0, w1 = y0_ref[...], y1_ref[...]
    moe = jnp.concatenate([_unpack_lo(w0) + _unpack_lo(w1), _unpack_hi(w0) + _unpack_hi(w1)], axis=-1)
    h2 = h1_ref[0] + g2_ref[0] * moe
    o_ref[0] = _rms(h2, fg_ref[...]) if final_norm else h2


def _final(h1, y01, g2, final_g, final_norm):
    B, S, D = h1.shape
    tm = min(1024, S)
    nt = S // tm
    nblk = B * nt
    return pl.pallas_call(
        functools.partial(_final_kernel, final_norm=final_norm),
        out_shape=jax.ShapeDtypeStruct((B, S, D), F32),
        grid=(B, nt),
        in_specs=[pl.BlockSpec((1, tm, D), lambda b, t: (b, t, 0)),
                  pl.BlockSpec((tm, D // 2), lambda b, t: (b * nt + t, 0)),
                  pl.BlockSpec((tm, D // 2), lambda b, t: (nblk + b * nt + t, 0)),
                  pl.BlockSpec((1, 1, D), lambda b, t: (b, 0, 0)), pl.BlockSpec((1, D), lambda b, t: (0, 0))],
        out_specs=pl.BlockSpec((1, tm, D), lambda b, t: (b, t, 0)),
        compiler_params=_cparams(("parallel", "parallel")),
        name="final",
    )(h1, y01, y01, g2[:, None, :], final_g.reshape(1, D))


def kernel(x, c, w_mod, b_mod, norm1_g, w_in, conv_w, conv_b, w_rg_a, b_rg_a, w_rg_x, b_rg_x, lru_lambda, kv_norm_g, w_uk, w_uv, w_rnn_out, w_att_out, w_out, norm2_g, w_group, b_group, w_expert, b_expert, moe_w1, moe_w3, moe_w2, final_g):
    B, S, D = x.shape
    depth = w_mod.shape[0]
    d_rnn = conv_w.shape[2]
    kv_rank, _, head_dim = w_uk.shape[1:]
    d_att = N_HEADS * head_dim
    idx_dim = (w_in.shape[2] - 2 * d_rnn - d_att - kv_rank - IDX_HEADS - 2 * D) // (IDX_HEADS + 1)
    topk = min(TOPK_MAX, S // 4)
    idx_scale = idx_dim ** -0.5 * IDX_HEADS ** -0.5
    att_scale = head_dim ** -0.5
    h = x
    for l in range(depth):
        mod = _adaln(c, w_mod[l], b_mod[l])
        sh1, sc1, g1, sh2, sc2, g2 = jnp.split(mod, 6, axis=-1)
        xr, gr, qlat, ctab, qi, ki, wi, ga, gb = _inproj(h, sh1, sc1, norm1_g[l], w_in[l], w_uk[l], kv_norm_g[l],
                                                        (d_rnn, d_att, kv_rank, idx_dim))
        hg = _rglru(xr, gr, conv_w[l], conv_b[l], w_rg_a[l], b_rg_a[l], w_rg_x[l], b_rg_x[l], lru_lambda[l])
        s_pad = -(-S // KEY_TILE) * KEY_TILE
        if s_pad != S:
            ki = jnp.pad(ki, ((0, 0), (0, s_pad - S), (0, 0)))
        table = ctab.reshape(B * S, kv_rank // 2)
        qlat2 = qlat.reshape(B * S, -1)
        nch = SC_BATCH_CHUNKS if B % SC_BATCH_CHUNKS == 0 else 1
        nb = B // nch
        olat = None
        for ch in range(nch):
            b0 = ch * nb
            scores, params = _indexer(qi, ki, wi, topk, idx_scale, b0, nb)
            sel, csel = _select_gather(scores, params, table, topk, S, b0)
            olat = _sparse_attn(csel, sel, qlat2, S, topk, att_scale, b0 * S, olat)
        h1, xrow, route = _merge(olat, hg, ga, gb, h, g1, sh2, sc2, norm2_g[l], w_uv[l], w_att_out[l],
                                 w_rnn_out[l], w_out[l], w_group[l], b_group[l], w_expert[l], b_expert[l])
        n_exp = moe_w1.shape[1]
        tmm = MOE_TILE
        src, dest, tile_expert, n_valid = _moe_dispatch(route.reshape(B * S, LANES)[:, :2], n_exp, tmm,
                                                        SC_GATHER_GRANULE)
        xs = _sc_row_gather(xrow.reshape(B * S, -1), src)
        yp = _moe_grouped(xs, tile_expert, n_valid, moe_w1[l], moe_w3[l], moe_w2[l], tmm)
        y01 = _sc_row_gather(yp, jnp.concatenate([dest[:, 0], dest[:, 1]]))
        h = _final(h1, y01, g2, final_g, final_norm=(l == depth - 1))
    return h
```

```python
import functools

import jax
import jax.numpy as jnp
from jax import lax
from jax.experimental import pallas as pl
from jax.experimental.pallas import tpu as pltpu
from jax.experimental.pallas import tpu_sc as plsc

LRU_C = 8.0
CONV_W = 4
N_HEADS = 8
IDX_HEADS = 8
TOPK_MAX = 256
N_GROUPS = 4
EXP_PER_GROUP = 8
EPS = 1e-6

LANES = 128
SUBLANES = 8
SC_LANES = 16
VMEM_LIMIT = 56 * 1024 * 1024

Q_TILE = 128
KEY_TILE = 256
COUNT_TILE = 512
SC_CHUNK = 64
SC_WIN = 64
SC_GATHER_GRANULE = 2 * 32 * SC_WIN
MOE_TILE = 512
INT_MIN = -(2 ** 31)

F32 = jnp.float32
BF16 = jnp.bfloat16
I32 = jnp.int32
U32 = jnp.uint32


def _cparams(sem):
    return pltpu.CompilerParams(dimension_semantics=sem, vmem_limit_bytes=VMEM_LIMIT)


def _dot(a, b, **kw):
    return jnp.dot(a, b, preferred_element_type=F32, **kw)


def _dot_nt(a, b):
    return lax.dot_general(a, b, (((1,), (1,)), ((), ())), preferred_element_type=F32)


def _rms(xf, g):
    return xf * lax.rsqrt(jnp.mean(xf * xf, axis=-1, keepdims=True) + EPS) * g


def _mod_kernel(c_ref, w_ref, b_ref, o_ref):
    c = c_ref[...]
    o_ref[...] = _dot(jax.nn.silu(c), w_ref[...], precision=lax.Precision.HIGHEST) + b_ref[...]


def _adaln(c, w_mod, b_mod):
    B, D = c.shape
    N = w_mod.shape[1]
    tn = 1024
    return pl.pallas_call(
        _mod_kernel,
        out_shape=jax.ShapeDtypeStruct((B, N), F32),
        grid=(N // tn,),
        in_specs=[pl.BlockSpec((B, D), lambda j: (0, 0)),
                  pl.BlockSpec((D, tn), lambda j: (0, j)),
                  pl.BlockSpec((1, tn), lambda j: (0, j))],
        out_specs=pl.BlockSpec((B, tn), lambda j: (0, j)),
        compiler_params=_cparams(("arbitrary",)),
        name="adaln",
    )(c, w_mod, b_mod.reshape(1, N))


def _inproj_kernel(x_ref, sh_ref, sc_ref, g_ref, wxg_ref, wq_ref, wc_ref, wqi_ref, wki_ref, wwi_ref,
                   wmg_ref, wuk_ref, kvg_ref,
                   xr_ref, gr_ref, ql_ref, ct_ref, qi_ref, ki_ref, wi_ref, ga_ref, gb_ref, *, d_rnn, kv_rank, idx_dim):
    x = x_ref[0]
    u = _rms(x, g_ref[...]) * (1.0 + sc_ref[0]) + sh_ref[0]
    ub = u.astype(BF16)
    xg = _dot(ub, wxg_ref[...])
    xr_ref[0] = xg[:, :d_rnn].astype(BF16)
    gr_ref[0] = xg[:, d_rnn:].astype(BF16)
    q = _dot(ub, wq_ref[...]).astype(BF16)
    hd = q.shape[1] // N_HEADS
    for h in range(N_HEADS):
        ql_ref[0, :, h * kv_rank:(h + 1) * kv_rank] = _dot(q[:, h * hd:(h + 1) * hd], wuk_ref[h]).astype(BF16)
    cn = _rms(_dot(ub, wc_ref[...]), kvg_ref[...])
    half = kv_rank // 2
    lo = lax.bitcast_convert_type(cn[:, :half].astype(BF16).astype(F32), U32)
    hi = lax.bitcast_convert_type(cn[:, half:].astype(BF16).astype(F32), U32)
    ct_ref[0] = (lo >> 16) | (hi & jnp.uint32(0xFFFF0000))
    qi = _dot(ub, wqi_ref[...]).astype(BF16)
    for h in range(IDX_HEADS):
        qi_ref[0, h] = qi[:, h * idx_dim:(h + 1) * idx_dim]
    ki_ref[0] = _dot(ub, wki_ref[...])[:, :idx_dim].astype(BF16)
    wi_ref[0] = _dot_nt(wwi_ref[...], ub)
    mg = jax.nn.sigmoid(_dot(ub, wmg_ref[...]))
    dm = mg.shape[1] // 2
    ga_ref[0] = mg[:, :dm].astype(BF16)
    gb_ref[0] = mg[:, dm:].astype(BF16)


def _inproj(x, sh1, sc1, norm1_g, w_in, w_uk, kv_norm_g, dims):
    B, S, D = x.shape
    d_rnn, d_att, kv_rank, idx_dim = dims
    tm = min(512, S)
    o = 0
    wxg = w_in[:, o:o + 2 * d_rnn].astype(BF16); o += 2 * d_rnn
    wq = w_in[:, o:o + d_att].astype(BF16); o += d_att
    wc = w_in[:, o:o + kv_rank].astype(BF16); o += kv_rank
    wqi = w_in[:, o:o + IDX_HEADS * idx_dim].astype(BF16); o += IDX_HEADS * idx_dim
    wki = jnp.pad(w_in[:, o:o + idx_dim], ((0, 0), (0, LANES - idx_dim))).astype(BF16); o += idx_dim
    wwi = w_in[:, o:o + IDX_HEADS].T.astype(BF16); o += IDX_HEADS
    wmg = w_in[:, o:].astype(BF16)
    hd = d_att // N_HEADS
    wuk = jnp.transpose(w_uk, (1, 2, 0)).astype(BF16)
    const2 = lambda b, t: (0, 0)
    const3 = lambda b, t: (0, 0, 0)
    row = lambda b, t: (b, t, 0)
    per_b = lambda b, t: (b, 0, 0)
    out_shapes = (
        jax.ShapeDtypeStruct((B, S, d_rnn), BF16),
        jax.ShapeDtypeStruct((B, S, d_rnn), BF16),
        jax.ShapeDtypeStruct((B, S, N_HEADS * kv_rank), BF16),
        jax.ShapeDtypeStruct((B, S, kv_rank // 2), U32),
        jax.ShapeDtypeStruct((B, IDX_HEADS, S, idx_dim), BF16),
        jax.ShapeDtypeStruct((B, S, idx_dim), BF16),
        jax.ShapeDtypeStruct((B, IDX_HEADS, S), F32),
        jax.ShapeDtypeStruct((B, S, D), BF16),
        jax.ShapeDtypeStruct((B, S, D), BF16),
    )
    out_specs = (
        pl.BlockSpec((1, tm, d_rnn), row),
        pl.BlockSpec((1, tm, d_rnn), row),
        pl.BlockSpec((1, tm, N_HEADS * kv_rank), row),
        pl.BlockSpec((1, tm, kv_rank // 2), row),
        pl.BlockSpec((1, IDX_HEADS, tm, idx_dim), lambda b, t: (b, 0, t, 0)),
        pl.BlockSpec((1, tm, idx_dim), row),
        pl.BlockSpec((1, IDX_HEADS, tm), lambda b, t: (b, 0, t)),
        pl.BlockSpec((1, tm, D), row),
        pl.BlockSpec((1, tm, D), row),
    )
    in_specs = [
        pl.BlockSpec((1, tm, D), row),
        pl.BlockSpec((1, 1, D), per_b),
        pl.BlockSpec((1, 1, D), per_b),
        pl.BlockSpec((1, D), const2),
        pl.BlockSpec(wxg.shape, const2),
        pl.BlockSpec(wq.shape, const2),
        pl.BlockSpec(wc.shape, const2),
        pl.BlockSpec(wqi.shape, const2),
        pl.BlockSpec(wki.shape, const2),
        pl.BlockSpec(wwi.shape, const2),
        pl.BlockSpec(wmg.shape, const2),
        pl.BlockSpec(wuk.shape, const3),
        pl.BlockSpec((1, kv_rank), const2),
    ]
    return pl.pallas_call(
        functools.partial(_inproj_kernel, d_rnn=d_rnn, kv_rank=kv_rank, idx_dim=idx_dim),
        out_shape=out_shapes,
        grid=(B, S // tm),
        in_specs=in_specs,
        out_specs=out_specs,
        compiler_params=_cparams(("parallel", "parallel")),
        name="inproj",
    )(x, sh1[:, None, :], sc1[:, None, :], norm1_g.reshape(1, D), wxg, wq, wc, wqi, wki, wwi, wmg, wuk,
      kv_norm_g.reshape(1, kv_rank))


def _rglru_kernel(xr_ref, gr_ref, cw_ref, cb_ref, wa_ref, ba_ref, wx_ref, bx_ref, lam_ref, o_ref, *, tt):
    seq, c = xr_ref.shape[1], xr_ref.shape[2]
    row = lax.broadcasted_iota(I32, (tt, c), 0)
    row8 = lax.broadcasted_iota(I32, (SUBLANES, c), 0)
    cw = cw_ref[...]
    cb = cb_ref[...]
    wa, wx = wa_ref[0], wx_ref[0]
    ba, bx = ba_ref[...], bx_ref[...]
    decay = -LRU_C * jax.nn.softplus(-lam_ref[...])

    def tile(t, carry):
        tail, h_prev = carry
        t0 = pl.multiple_of(t * tt, tt)
        x = xr_ref[0, pl.ds(t0, tt), :].astype(F32)
        xc = x * cw[CONV_W - 1:CONV_W, :] + cb
        for d in range(1, CONV_W):
            rolled = pltpu.roll(x, d, 0)
            head = jnp.where(row8 < d, pltpu.roll(tail, d, 0), rolled[:SUBLANES])
            xs = jnp.concatenate([head, rolled[SUBLANES:]], axis=0)
            xc = xc + xs * cw[CONV_W - 1 - d:CONV_W - d, :]
        xb = xc.astype(BF16)
        r = jax.nn.sigmoid(_dot(xb, wa) + ba)
        i = jax.nn.sigmoid(_dot(xb, wx) + bx)
        log_a = decay * r
        a = jnp.exp(log_a)
        b = jnp.sqrt(1.0 - a * a) * (i * xc)
        d = 1
        while d < tt:
            a_s = jnp.where(row >= d, pltpu.roll(a, d, 0), 1.0)
            b_s = jnp.where(row >= d, pltpu.roll(b, d, 0), 0.0)
            b = a * b_s + b
            a = a * a_s
            d *= 2
        h = a * h_prev + b
        o_ref[0, pl.ds(t0, tt), :] = (jax.nn.gelu(gr_ref[0, pl.ds(t0, tt), :].astype(F32)) * h).astype(BF16)
        return x[tt - SUBLANES:], h[tt - 1:tt, :]

    lax.fori_loop(0, seq // tt, tile, (jnp.zeros((SUBLANES, c), F32), jnp.zeros((1, c), F32)))


def _rglru(xr, gr, conv_w, conv_b, w_a, b_a, w_x, b_x, lam):
    B, S, C = xr.shape
    nb, bw = w_a.shape[0], w_a.shape[1]
    tt = min(256, S)
    blk = lambda b, n: (b, 0, n)
    vec = lambda b, n: (0, n)
    mat = lambda b, n: (n, 0, 0)
    return pl.pallas_call(
        functools.partial(_rglru_kernel, tt=tt),
        out_shape=jax.ShapeDtypeStruct((B, S, C), BF16),
        grid=(B, nb),
        in_specs=[pl.BlockSpec((1, S, bw), blk), pl.BlockSpec((1, S, bw), blk),
                  pl.BlockSpec((CONV_W, bw), vec), pl.BlockSpec((1, bw), vec),
                  pl.BlockSpec((1, bw, bw), mat), pl.BlockSpec((1, bw), vec),
                  pl.BlockSpec((1, bw, bw), mat), pl.BlockSpec((1, bw), vec),
                  pl.BlockSpec((1, bw), vec)],
        out_specs=pl.BlockSpec((1, S, bw), blk),
        compiler_params=_cparams(("parallel", "parallel")),
        name="rglru",
    )(xr, gr, conv_w, conv_b.reshape(1, C), w_a.astype(BF16), b_a.reshape(1, C), w_x.astype(BF16),
      b_x.reshape(1, C), lam.reshape(1, C))


def _sortable(x):
    b = lax.bitcast_convert_type(x, I32)
    return b ^ ((b >> 31) & jnp.int32(0x7FFFFFFF))


def _indexer_kernel(qi_ref, ki_ref, wi_ref, sc_ref, par_ref, *, topk, idx_scale):
    i = pl.program_id(1)
    n_keys = (i + 1) * Q_TILE
    n_tiles = (n_keys + KEY_TILE - 1) // KEY_TILE
    qi = qi_ref[0].reshape(IDX_HEADS * Q_TILE, qi_ref.shape[-1])
    w = wi_ref[0]
    qpos = i * Q_TILE + lax.broadcasted_iota(I32, (1, Q_TILE), 1)
    key_rows = lax.broadcasted_iota(I32, (KEY_TILE, Q_TILE), 0)

    def score_tile(kt, carry):
        k0 = pl.multiple_of(kt * KEY_TILE, KEY_TILE)
        rel = jnp.maximum(_dot_nt(ki_ref[0, pl.ds(k0, KEY_TILE), :], qi), 0.0)
        s = rel[:, :Q_TILE] * w[0:1, :]
        for h in range(1, IDX_HEADS):
            s = s + rel[:, h * Q_TILE:(h + 1) * Q_TILE] * w[h:h + 1, :]
        s = s * idx_scale
        s = jnp.where(key_rows + k0 <= qpos, s, -jnp.inf)
        sc_ref[0, 0, pl.ds(k0, KEY_TILE), :] = _sortable(s)
        return carry

    lax.fori_loop(0, n_tiles, score_tile, 0)

    n_ctiles = (n_tiles * KEY_TILE + COUNT_TILE - 1) // COUNT_TILE

    @pl.when(n_ctiles * COUNT_TILE > n_tiles * KEY_TILE)
    def _():
        sc_ref[0, 0, pl.ds(pl.multiple_of(n_tiles * KEY_TILE, KEY_TILE), KEY_TILE), :] = jnp.full(
            (KEY_TILE, Q_TILE), INT_MIN, I32)

    rows = lax.broadcasted_iota(I32, (COUNT_TILE, Q_TILE), 0)

    def count(pred):
        def body(c, acc):
            r0 = pl.multiple_of(c * COUNT_TILE, COUNT_TILE)
            x = sc_ref[0, 0, pl.ds(r0, COUNT_TILE), :]
            m = pred(x, r0).astype(I32)
            return acc + jnp.sum(m.reshape(COUNT_TILE // SUBLANES, SUBLANES, Q_TILE), axis=0)
        acc = lax.fori_loop(0, n_ctiles, body, jnp.zeros((SUBLANES, Q_TILE), I32))
        return jnp.sum(acc, axis=0, keepdims=True)

    short = qpos + 1 <= topk
    s_total = sc_ref.shape[2]

    def bit_step(j, state):
        t, n_t = state
        bit = 31 - j
        cand = jnp.where(bit == 31, jnp.zeros_like(t), t + (jnp.int32(1) << jnp.minimum(bit, 30)))
        n_c = count(lambda x, r0: x >= cand)
        ok = n_c >= topk
        return jnp.where(ok, cand, t), jnp.where(ok, n_c, n_t)

    t_init = jnp.full((1, Q_TILE), INT_MIN, I32)
    t_ge, n_ge = lax.fori_loop(0, 32, bit_step, (t_init, jnp.full((1, Q_TILE), s_total, I32)))
    tied = jnp.logical_and(n_ge > topk, jnp.logical_not(short))

    def tie_cut(_):
        need = topk - count(lambda x, r0: x > t_ge)

        def step(j, c):
            bit = jnp.int32(1) << (jnp.int32(s_total.bit_length() - 1) - j)
            cand = c + bit
            below = count(lambda x, r0: jnp.logical_and(x == t_ge, rows + r0 < cand))
            return jnp.where(below < need, cand, c)
        return lax.fori_loop(0, s_total.bit_length(), step, jnp.zeros((1, Q_TILE), I32))

    any_tied = jnp.max(tied.astype(I32)) > 0
    cut = lax.cond(any_tied, tie_cut, lambda _: jnp.zeros((1, Q_TILE), I32), 0)
    tau = jnp.where(tied, t_ge, t_ge - 1)
    cut = jnp.where(tied, cut, -1)
    neg_inf_key = _sortable(jnp.full((1, Q_TILE), -jnp.inf, F32))
    tau = jnp.where(short, neg_inf_key, tau)
    cut = jnp.where(short, -1, cut)
    par_ref[0, 0] = jnp.concatenate([tau, cut, jnp.zeros((SUBLANES - 2, Q_TILE), I32)], axis=0)


def _indexer(qi, ki, wi, topk, idx_scale, b0, nb):
    _, H, S, di = qi.shape
    nq = S // Q_TILE
    s_pad = ki.shape[1]
    return pl.pallas_call(
        functools.partial(_indexer_kernel, topk=topk, idx_scale=idx_scale),
        out_shape=(jax.ShapeDtypeStruct((nb, nq, s_pad, Q_TILE), I32),
                   jax.ShapeDtypeStruct((nb, nq, SUBLANES, Q_TILE), I32)),
        grid=(nb, nq),
        in_specs=[pl.BlockSpec((1, H, Q_TILE, di), lambda b, i: (b0 + b, 0, i, 0)),
                  pl.BlockSpec((1, s_pad, di), lambda b, i: (b0 + b, 0, 0)),
                  pl.BlockSpec((1, H, Q_TILE), lambda b, i: (b0 + b, 0, i))],
        out_specs=(pl.BlockSpec((1, 1, s_pad, Q_TILE), lambda b, i: (b, i, 0, 0)),
                   pl.BlockSpec((1, 1, SUBLANES, Q_TILE), lambda b, i: (b, i, 0, 0))),
        compiler_params=_cparams(("parallel", "arbitrary")),
        name="indexer",
    )(qi, ki, wi)


def _select_gather(scores, params, table, topk, seq, b0):
    nb, nq, s_pad, _ = scores.shape
    W = table.shape[1]
    info = plsc.get_sparse_core_info()
    nc, ns = info.num_cores, info.num_subcores
    nw = nc * ns
    n_units = nb * nq
    assert nw % nb == 0 and nq % (nw // nb) == 0 and topk % LANES == 0 and Q_TILE % (2 * SC_CHUNK) == 0
    stride = nw // nb
    units_per_w = n_units // nw
    n_groups = Q_TILE // SC_LANES
    idx_rows = Q_TILE * topk // LANES
    rows_per_q = topk // LANES
    mesh = plsc.VectorSubcoreMesh(core_axis_name="c", subcore_axis_name="s")

    @functools.partial(
        pl.kernel, mesh=mesh,
        out_type=(jax.ShapeDtypeStruct((n_units, idx_rows, LANES), I32),
                  jax.ShapeDtypeStruct((n_units * Q_TILE, topk, W), U32)),
        scratch_types=[
            pltpu.VMEM((2, SC_CHUNK, Q_TILE), I32),
            pltpu.VMEM((SUBLANES, Q_TILE), I32),
            pltpu.VMEM((idx_rows, LANES), I32),
            pltpu.VMEM((2, topk, W), U32),
            pltpu.SemaphoreType.DMA((2,)),
            pltpu.SemaphoreType.DMA((2,)),
            pltpu.SemaphoreType.DMA((2,)),
        ],
        compiler_params=pltpu.CompilerParams(needs_layout_passes=False),
        name="select_gather",
    )
    def k(sc_hbm, par_hbm, tab_hbm, sel_hbm, out_hbm, buf_v, par_v, idx_v, rows_v, sem_c, sem_g, sem_w):
        wid = lax.axis_index("s") * nc + lax.axis_index("c")
        lane = lax.iota(I32, SC_LANES)

        def chunk_copy(unit, c, slot):
            return pltpu.make_async_copy(sc_hbm.at[unit, pl.ds(c * SC_CHUNK, SC_CHUNK)], buf_v.at[slot],
                                         sem_c.at[slot])

        def gather_copies(q, slot):
            return [pltpu.make_async_copy(tab_hbm.at[idx_v.at[q * rows_per_q + p]],
                                          rows_v.at[slot, pl.ds(p * LANES, LANES)], sem_g.at[slot])
                    for p in range(rows_per_q)]

        def write_copy(unit, q, slot):
            return pltpu.make_async_copy(rows_v.at[slot], out_hbm.at[unit * Q_TILE + q], sem_w.at[slot])

        @pl.loop(0, units_per_w)
        def _(uu):
            b = wid % nb
            i = wid // nb + stride * uu
            unit = b * nq + i
            base = (b0 + b) * seq
            n_chunks = (i + 1) * (Q_TILE // SC_CHUNK)
            chunk_copy(unit, 0, 0).start()
            pltpu.sync_copy(par_hbm.at[unit], par_v)
            fill = jnp.zeros((SC_LANES,), I32) + base

            @pl.loop(0, idx_rows)
            def _(j):
                for g in range(LANES // SC_LANES):
                    idx_v[j, pl.ds(g * SC_LANES, SC_LANES)] = fill + ((lane + g * SC_LANES + j * LANES) & (topk - 1))

            taus = [par_v[0, pl.ds(g * SC_LANES, SC_LANES)] for g in range(n_groups)]
            cuts = [par_v[1, pl.ds(g * SC_LANES, SC_LANES)] for g in range(n_groups)]
            slot0 = [(lane + g * SC_LANES) * topk for g in range(n_groups)]

            def process(c, slot, cnts):
                def row_body(r, cnts):
                    kv = jnp.zeros((SC_LANES,), I32) + (c * SC_CHUNK + r)
                    gv = kv + base
                    ss = [buf_v[slot, r, pl.ds(g * SC_LANES, SC_LANES)] for g in range(n_groups)]
                    ms = []
                    for g in range(n_groups):
                        m = (ss[g] > taus[g]) | ((ss[g] == taus[g]) & (kv <= cuts[g]))
                        ms.append(m & (cnts[g] < topk))
                    out = []
                    for g in range(n_groups):
                        pos = slot0[g] + cnts[g]
                        plsc.store_scatter(idx_v, [pos >> 7, pos & (LANES - 1)], gv, mask=ms[g])
                        out.append(cnts[g] + ms[g].astype(I32))
                    return tuple(out)

                return plsc.parallel_loop(0, SC_CHUNK, unroll=2, carry=cnts)(row_body)

            def pair_body(p, cnts):
                c0 = 2 * p
                chunk_copy(unit, c0, 0).wait()
                chunk_copy(unit, c0 + 1, 1).start()
                cnts = process(c0, 0, cnts)
                chunk_copy(unit, c0 + 1, 1).wait()

                @pl.when(c0 + 2 < n_chunks)
                def _():
                    chunk_copy(unit, c0 + 2, 0).start()

                return process(c0 + 1, 1, cnts)

            zero = jnp.zeros((SC_LANES,), I32)
            lax.fori_loop(0, n_chunks // 2, pair_body, tuple(zero for _ in range(n_groups)))
            pltpu.sync_copy(idx_v, sel_hbm.at[unit])

            for cp in gather_copies(0, 0):
                cp.start()

            @pl.loop(0, Q_TILE // 2)
            def _(j):
                for s in range(2):
                    q = 2 * j + s
                    for cp in gather_copies(q, s):
                        cp.wait()
                    write_copy(unit, q, s).start()

                    @pl.when(q > 0)
                    def _():
                        write_copy(unit, q - 1, 1 - s).wait()

                    @pl.when(q + 1 < Q_TILE)
                    def _():
                        for cp in gather_copies(q + 1, 1 - s):
                            cp.start()

            write_copy(unit, Q_TILE - 1, 1).wait()

    sel, rows = k(scores.reshape(n_units, s_pad, Q_TILE), params.reshape(n_units, SUBLANES, Q_TILE), table)
    return sel.reshape(n_units * Q_TILE, topk), rows


def _attn_kernel(cs_ref, sel_ref, ql_ref, sl_ref, o_ref, *, seq, topk, att_scale, row0):
    tq = cs_ref.shape[0]
    t0 = row0 + pl.program_id(0) * tq
    w = cs_ref[...]
    lo = lax.bitcast_convert_type(w << 16, F32).astype(BF16)
    hi = lax.bitcast_convert_type(w & jnp.uint32(0xFFFF0000), F32).astype(BF16)
    g = jnp.concatenate([lo, hi], axis=-1)
    r = g.shape[-1]
    ql = ql_ref[...].reshape(tq, N_HEADS, r)
    logits = jnp.einsum("qhr,qkr->qhk", ql, g, preferred_element_type=F32) * att_scale
    row = t0 + lax.broadcasted_iota(I32, (tq, 1, 1), 0)
    base = (row // seq) * seq
    qpos = row - base
    sel = sel_ref[...].reshape(tq, 1, topk) - base
    logits = logits - sl_ref[...].reshape(1, N_HEADS, 1) * (qpos - sel).astype(F32)
    slot = lax.broadcasted_iota(I32, (1, 1, topk), 2)
    logits = jnp.where(slot < jnp.minimum(qpos + 1, topk), logits, -jnp.inf)
    p = jax.nn.softmax(logits, axis=-1)
    o = jnp.einsum("qhk,qkr->qhr", p.astype(BF16), g, preferred_element_type=F32)
    o_ref[...] = o.reshape(tq, N_HEADS * r).astype(BF16)


def _attn_kernel_inplace(cs_ref, sel_ref, ql_ref, sl_ref, prev_ref, o_ref, **kw):
    del prev_ref
    _attn_kernel(cs_ref, sel_ref, ql_ref, sl_ref, o_ref, **kw)


def _sparse_attn(csel, sel, qlat, seq, topk, att_scale, row0, olat_prev):
    T, _, wh = csel.shape
    tq = 16
    hr = qlat.shape[1]
    blk0 = row0 // tq
    slopes = (2.0 ** (-8.0 * jnp.arange(1, N_HEADS + 1, dtype=F32) / N_HEADS)).reshape(N_HEADS, 1)
    kw = dict(seq=seq, topk=topk, att_scale=att_scale, row0=row0)
    in_specs = [pl.BlockSpec((tq, topk, wh), lambda i: (i, 0, 0)),
                pl.BlockSpec((tq, topk), lambda i: (i, 0)),
                pl.BlockSpec((tq, hr), lambda i: (blk0 + i, 0)),
                pl.BlockSpec((N_HEADS, 1), lambda i: (0, 0))]
    args = [csel, sel, qlat, slopes]
    if olat_prev is None:
        body, aliases = functools.partial(_attn_kernel, **kw), {}
    else:
        body, aliases = functools.partial(_attn_kernel_inplace, **kw), {len(args): 0}
        in_specs.append(pl.BlockSpec(memory_space=pl.ANY))
        args.append(olat_prev)
    return pl.pallas_call(
        body,
        out_shape=jax.ShapeDtypeStruct(qlat.shape, BF16),
        grid=(T // tq,),
        in_specs=in_specs,
        out_specs=pl.BlockSpec((tq, hr), lambda i: (blk0 + i, 0)),
        input_output_aliases=aliases,
        compiler_params=_cparams(("parallel",)),
        name="sparse_attn",
    )(*args)


def _pack_halves(x):
    n = x.shape[1] // 2
    xb = x.astype(BF16).astype(F32)
    return (lax.bitcast_convert_type(xb[:, :n], U32) >> 16) | (lax.bitcast_convert_type(xb[:, n:], U32)
                                                              & jnp.uint32(0xFFFF0000))


def _unpack_lo(w):
    return lax.bitcast_convert_type(w << 16, F32)


def _unpack_hi(w):
    return lax.bitcast_convert_type(w & jnp.uint32(0xFFFF0000), F32)


def _merge_kernel(ol_ref, hg_ref, ga_ref, gb_ref, x_ref, g1_ref, sh_ref, sc_ref, n2_ref, wuv_ref, wao_ref,
                  wro_ref, wo_ref, wrt_ref, brt_ref, h1_ref, xrow_ref, route_ref):
    ol = ol_ref[0]
    r = wuv_ref.shape[1]
    att = jnp.concatenate([_dot(ol[:, h * r:(h + 1) * r], wuv_ref[h]) for h in range(N_HEADS)], axis=-1)
    y_b = _dot(att.astype(BF16), wao_ref[...])
    y_a = _dot(hg_ref[0], wro_ref[...])
    mixin = ga_ref[0].astype(F32) * y_a + gb_ref[0].astype(F32) * y_b
    mix = _dot(mixin.astype(BF16), wo_ref[...])
    h1 = x_ref[0] + g1_ref[0] * mix
    h1_ref[0] = h1
    u2 = _rms(h1, n2_ref[...]) * (1.0 + sc_ref[0]) + sh_ref[0]
    lg = _dot(u2, wrt_ref[...], precision=lax.Precision.HIGHEST) + brt_ref[...]
    n_exp = N_GROUPS * EXP_PER_GROUP
    lane = lax.broadcasted_iota(I32, lg.shape, 1)
    big = jnp.int32(LANES)
    is_g = jnp.logical_and(lane >= n_exp, lane < n_exp + N_GROUPS)
    gl = jnp.where(is_g, lg, -jnp.inf)
    gmax = jnp.max(gl, axis=-1, keepdims=True)
    g_sel = jnp.min(jnp.where(jnp.logical_and(is_g, gl == gmax), lane, big), axis=-1, keepdims=True) - n_exp
    g_w = 1.0 / jnp.sum(jnp.where(is_g, jnp.exp(gl - gmax), 0.0), axis=-1, keepdims=True)
    in_grp = jnp.logical_and(lane >= g_sel * EXP_PER_GROUP, lane < (g_sel + 1) * EXP_PER_GROUP)
    el = jnp.where(in_grp, lg, -jnp.inf)
    e1 = jnp.max(el, axis=-1, keepdims=True)
    i1 = jnp.min(jnp.where(jnp.logical_and(in_grp, el == e1), lane, big), axis=-1, keepdims=True)
    el2 = jnp.where(lane == i1, -jnp.inf, el)
    e2 = jnp.max(el2, axis=-1, keepdims=True)
    i2 = jnp.min(jnp.where(jnp.logical_and(in_grp, el2 == e2), lane, big), axis=-1, keepdims=True)
    x2 = jnp.exp(e2 - e1)
    den = 1.0 + x2
    gates = jnp.where(lane == i1, g_w * (1.0 / den), 0.0) + jnp.where(lane == i2, g_w * (x2 / den), 0.0)
    gate_words = lax.bitcast_convert_type(gates.astype(BF16).astype(F32), U32) >> 16
    xrow_ref[0] = jnp.concatenate([_pack_halves(u2), gate_words], axis=-1)
    route_ref[0] = jnp.where(lane == 0, i1, jnp.where(lane == 1, i2, 0))


def _merge(olat, hg, ga, gb, x, g1, sh2, sc2, norm2_g, w_uv, w_att_out, w_rnn_out, w_out, w_group, b_group,
           w_expert, b_expert):
    B, S, D = x.shape
    tm = min(512, S)
    n_exp = w_expert.shape[1]
    wuv = jnp.transpose(w_uv, (1, 0, 2)).astype(BF16)
    wrt = jnp.pad(jnp.concatenate([w_expert, w_group], axis=1), ((0, 0), (0, LANES - n_exp - N_GROUPS)))
    brt = jnp.pad(jnp.concatenate([b_expert, b_group]), (0, LANES - n_exp - N_GROUPS)).reshape(1, LANES)
    row = lambda b, t: (b, t, 0)
    per_b = lambda b, t: (b, 0, 0)
    const2 = lambda b, t: (0, 0)
    const3 = lambda b, t: (0, 0, 0)
    olat = olat.reshape(B, S, -1)
    wao, wro, wo = w_att_out.astype(BF16), w_rnn_out.astype(BF16), w_out.astype(BF16)
    return pl.pallas_call(
        _merge_kernel,
        out_shape=(jax.ShapeDtypeStruct((B, S, D), F32), jax.ShapeDtypeStruct((B, S, D // 2 + LANES), U32),
                   jax.ShapeDtypeStruct((B, S, LANES), I32)),
        grid=(B, S // tm),
        in_specs=[pl.BlockSpec((1, tm, olat.shape[-1]), row), pl.BlockSpec((1, tm, hg.shape[-1]), row),
                  pl.BlockSpec((1, tm, D), row), pl.BlockSpec((1, tm, D), row), pl.BlockSpec((1, tm, D), row),
                  pl.BlockSpec((1, 1, D), per_b), pl.BlockSpec((1, 1, D), per_b), pl.BlockSpec((1, 1, D), per_b),
                  pl.BlockSpec((1, D), const2), pl.BlockSpec(wuv.shape, const3), pl.BlockSpec(wao.shape, const2),
                  pl.BlockSpec(wro.shape, const2), pl.BlockSpec(wo.shape, const2), pl.BlockSpec(wrt.shape, const2),
                  pl.BlockSpec((1, LANES), const2)],
        out_specs=(pl.BlockSpec((1, tm, D), row), pl.BlockSpec((1, tm, D // 2 + LANES), row),
                   pl.BlockSpec((1, tm, LANES), row)),
        compiler_params=_cparams(("parallel", "parallel")),
        name="merge",
    )(olat, hg, ga, gb, x, g1[:, None, :], sh2[:, None, :], sc2[:, None, :], norm2_g.reshape(1, D), wuv, wao, wro,
      wo, wrt, brt)


def _sc_row_gather(table, idx):
    _, W = table.shape
    M = idx.shape[0]
    info = plsc.get_sparse_core_info()
    nc, ns = info.num_cores, info.num_subcores
    nw = nc * ns
    assert M % (nw * SC_WIN * 2) == 0
    n_win = M // (nw * SC_WIN)
    mesh = plsc.VectorSubcoreMesh(core_axis_name="c", subcore_axis_name="s")

    @functools.partial(
        pl.kernel, mesh=mesh,
        out_type=jax.ShapeDtypeStruct((M, W), U32),
        scratch_types=[pltpu.VMEM((2, SC_WIN), I32), pltpu.VMEM((2, SC_WIN, W), U32),
                       pltpu.SemaphoreType.DMA((2,)), pltpu.SemaphoreType.DMA((2,))],
        compiler_params=pltpu.CompilerParams(needs_layout_passes=False),
        name="row_gather",
    )
    def k(tab_hbm, idx_hbm, out_hbm, idx_v, rows_v, sem_g, sem_w):
        wid = lax.axis_index("s") * nc + lax.axis_index("c")

        def win_id(j):
            return j * nw + wid

        def gather(slot):
            return pltpu.make_async_copy(tab_hbm.at[idx_v.at[slot]], rows_v.at[slot], sem_g.at[slot])

        def write(j, slot):
            return pltpu.make_async_copy(rows_v.at[slot], out_hbm.at[pl.ds(win_id(j) * SC_WIN, SC_WIN)],
                                         sem_w.at[slot])

        pltpu.sync_copy(idx_hbm.at[win_id(0)], idx_v.at[0])
        gather(0).start()

        @pl.loop(0, n_win // 2)
        def _(jj):
            for s in range(2):
                j = 2 * jj + s
                gather(s).wait()
                write(j, s).start()

                @pl.when(j > 0)
                def _():
                    write(j - 1, 1 - s).wait()

                @pl.when(j + 1 < n_win)
                def _():
                    pltpu.sync_copy(idx_hbm.at[win_id(j + 1)], idx_v.at[1 - s])
                    gather(1 - s).start()

        write(n_win - 1, 1).wait()

    return k(table, idx.reshape(M // SC_WIN, SC_WIN))


def _moe_dispatch(e12, n_exp, tmm, granule):
    T = e12.shape[0]
    A = 2 * T
    a_pad = -(-(A + n_exp * tmm) // granule) * granule
    n_tiles = a_pad // tmm
    eid = e12.reshape(A)
    onehot = (eid[:, None] == jnp.arange(n_exp, dtype=I32)[None, :]).astype(I32)
    csum = jnp.cumsum(onehot, axis=0)
    rank = jnp.sum(onehot * csum, axis=1) - 1
    counts = csum[-1]
    padded = (counts + tmm - 1) // tmm * tmm
    pend = jnp.cumsum(padded)
    dest = (pend - padded)[eid] + rank
    src = (jnp.arange(a_pad, dtype=I32) % T).at[dest].set(jnp.arange(A, dtype=I32) // 2)
    tile_expert = jnp.minimum(jnp.searchsorted(pend, jnp.arange(n_tiles, dtype=I32) * tmm, side="right"),
                              n_exp - 1).astype(I32)
    n_valid = (pend[-1] // tmm).astype(I32).reshape(1)
    return src, dest.reshape(T, 2).astype(I32), tile_expert, n_valid


def _moe_grouped_kernel(te_ref, nv_ref, xs_ref, w1_ref, w3_ref, w2_ref, y_ref, w1b_ref, w3b_ref, w2b_ref):
    i = pl.program_id(0)

    @pl.when(jnp.logical_or(i == 0, te_ref[i] != te_ref[jnp.maximum(i - 1, 0)]))
    def _():
        w1b_ref[...] = w1_ref[0].astype(BF16)
        w3b_ref[...] = w3_ref[0].astype(BF16)
        w2b_ref[...] = w2_ref[0].astype(BF16)

    @pl.when(i < nv_ref[0])
    def _():
        w = xs_ref[...]
        half = y_ref.shape[1]
        wu = w[:, :half]
        u = jnp.concatenate([_unpack_lo(wu).astype(BF16), _unpack_hi(wu).astype(BF16)], axis=-1)
        gates = _unpack_lo(w[:, half:])
        lane = lax.broadcasted_iota(I32, gates.shape, 1)
        gcol = jnp.sum(jnp.where(lane == te_ref[i], gates, 0.0), axis=-1, keepdims=True)
        hid = jax.nn.silu(_dot(u, w1b_ref[...])) * _dot(u, w3b_ref[...])
        y_ref[...] = _pack_halves(_dot(hid.astype(BF16), w2b_ref[...]) * gcol)


def _moe_grouped(xs, tile_expert, n_valid, w1, w3, w2, tmm):
    a_pad, W = xs.shape
    n_exp, D, de = w1.shape
    wmap = lambda i, te, nv: (te[i], 0, 0)
    return pl.pallas_call(
        _moe_grouped_kernel,
        out_shape=jax.ShapeDtypeStruct((a_pad, D // 2), U32),
        grid_spec=pltpu.PrefetchScalarGridSpec(
            num_scalar_prefetch=2, grid=(a_pad // tmm,),
            in_specs=[pl.BlockSpec((tmm, W), lambda i, te, nv: (i, 0)),
                      pl.BlockSpec((1, D, de), wmap), pl.BlockSpec((1, D, de), wmap), pl.BlockSpec((1, de, D), wmap)],
            out_specs=pl.BlockSpec((tmm, D // 2), lambda i, te, nv: (i, 0)),
            scratch_shapes=[pltpu.VMEM((D, de), BF16), pltpu.VMEM((D, de), BF16), pltpu.VMEM((de, D), BF16)]),
        compiler_params=_cparams(("arbitrary",)),
        name="moe_grouped",
    )(tile_expert, n_valid, xs, w1, w3, w2)


def _final_kernel(h1_ref, y0_ref, y1_ref, g2_ref, fg_ref, o_ref, *, final_norm):
    w0, w1 = y0_ref[...], y1_ref[...]
    moe = jnp.concatenate([_unpack_lo(w0) + _unpack_lo(w1), _unpack_hi(w0) + _unpack_hi(w1)], axis=-1)
    h2 = h1_ref[0] + g2_ref[0] * moe
    o_ref[0] = _rms(h2, fg_ref[...]) if final_norm else h2


def _final(h1, y01, g2, final_g, final_norm):
    B, S, D = h1.shape
    tm = min(1024, S)
    nt = S // tm
    nblk = B * nt
    return pl.pallas_call(
        functools.partial(_final_kernel, final_norm=final_norm),
        out_shape=jax.ShapeDtypeStruct((B, S, D), F32),
        grid=(B, nt),
        in_specs=[pl.BlockSpec((1, tm, D), lambda b, t: (b, t, 0)),
                  pl.BlockSpec((tm, D // 2), lambda b, t: (b * nt + t, 0)),
                  pl.BlockSpec((tm, D // 2), lambda b, t: (nblk + b * nt + t, 0)),
                  pl.BlockSpec((1, 1, D), lambda b, t: (b, 0, 0)), pl.BlockSpec((1, D), lambda b, t: (0, 0))],
        out_specs=pl.BlockSpec((1, tm, D), lambda b, t: (b, t, 0)),
        compiler_params=_cparams(("parallel", "parallel")),
        name="final",
    )(h1, y01, y01, g2[:, None, :], final_g.reshape(1, D))


def _batch_chunks(n_batch):
    if n_batch < 4 or n_batch % 2:
        return [1] * n_batch
    return [2] * (n_batch // 2 - 1) + [1, 1]


def kernel(x, c, w_mod, b_mod, norm1_g, w_in, conv_w, conv_b, w_rg_a, b_rg_a, w_rg_x, b_rg_x, lru_lambda, kv_norm_g, w_uk, w_uv, w_rnn_out, w_att_out, w_out, norm2_g, w_group, b_group, w_expert, b_expert, moe_w1, moe_w3, moe_w2, final_g):
    B, S, D = x.shape
    depth = w_mod.shape[0]
    d_rnn = conv_w.shape[2]
    kv_rank, _, head_dim = w_uk.shape[1:]
    d_att = N_HEADS * head_dim
    idx_dim = (w_in.shape[2] - 2 * d_rnn - d_att - kv_rank - IDX_HEADS - 2 * D) // (IDX_HEADS + 1)
    topk = min(TOPK_MAX, S // 4)
    idx_scale = idx_dim ** -0.5 * IDX_HEADS ** -0.5
    att_scale = head_dim ** -0.5
    h = x
    for l in range(depth):
        mod = _adaln(c, w_mod[l], b_mod[l])
        sh1, sc1, g1, sh2, sc2, g2 = jnp.split(mod, 6, axis=-1)
        xr, gr, qlat, ctab, qi, ki, wi, ga, gb = _inproj(h, sh1, sc1, norm1_g[l], w_in[l], w_uk[l], kv_norm_g[l],
                                                        (d_rnn, d_att, kv_rank, idx_dim))
        s_pad = -(-S // COUNT_TILE) * COUNT_TILE
        if s_pad != S:
            ki = jnp.pad(ki, ((0, 0), (0, s_pad - S), (0, 0)))
        table = ctab.reshape(B * S, kv_rank // 2)
        qlat2 = qlat.reshape(B * S, -1)
        olat, b0 = None, 0
        for nb in _batch_chunks(B):
            scores, params = _indexer(qi, ki, wi, topk, idx_scale, b0, nb)
            sel, csel = _select_gather(scores, params, table, topk, S, b0)
            olat = _sparse_attn(csel, sel, qlat2, S, topk, att_scale, b0 * S, olat)
            b0 += nb
        hg = _rglru(xr, gr, conv_w[l], conv_b[l], w_rg_a[l], b_rg_a[l], w_rg_x[l], b_rg_x[l], lru_lambda[l])
        h1, xrow, route = _merge(olat, hg, ga, gb, h, g1, sh2, sc2, norm2_g[l], w_uv[l], w_att_out[l],
                                 w_rnn_out[l], w_out[l], w_group[l], b_group[l], w_expert[l], b_expert[l])
        n_exp = moe_w1.shape[1]
        tmm = MOE_TILE
        src, dest, tile_expert, n_valid = _moe_dispatch(route.reshape(B * S, LANES)[:, :2], n_exp, tmm,
                                                        SC_GATHER_GRANULE)
        xs = _sc_row_gather(xrow.reshape(B * S, -1), src)
        yp = _moe_grouped(xs, tile_expert, n_valid, moe_w1[l], moe_w3[l], moe_w2[l], tmm)
        y01 = _sc_row_gather(yp, jnp.concatenate([dest[:, 0], dest[:, 1]]))
        h = _final(h1, y01, g2, final_g, final_norm=(l == depth - 1))
    return h
```

```python
import functools

import jax
import jax.numpy as jnp
from jax import lax
from jax.experimental import pallas as pl
from jax.experimental.pallas import tpu as pltpu
from jax.experimental.pallas import tpu_sc as plsc

LRU_C = 8.0
CONV_W = 4
N_HEADS = 8
IDX_HEADS = 8
TOPK_MAX = 256
N_GROUPS = 4
EXP_PER_GROUP = 8
EPS = 1e-6

LANES = 128
SUBLANES = 8
SC_LANES = 16
VMEM_LIMIT = 56 * 1024 * 1024

Q_TILE = 128
KEY_TILE = 256
COUNT_TILE = 512
SC_CHUNK = 64
SC_WIN = 64
SC_GATHER_GRANULE = 2 * 32 * SC_WIN
MOE_TILE = 512
INT_MIN = -(2 ** 31)

F32 = jnp.float32
BF16 = jnp.bfloat16
I32 = jnp.int32
U32 = jnp.uint32


def _cparams(sem):
    return pltpu.CompilerParams(dimension_semantics=sem, vmem_limit_bytes=VMEM_LIMIT)


def _dot(a, b, **kw):
    return jnp.dot(a, b, preferred_element_type=F32, **kw)


def _dot_nt(a, b):
    return lax.dot_general(a, b, (((1,), (1,)), ((), ())), preferred_element_type=F32)


def _rms(xf, g):
    return xf * lax.rsqrt(jnp.mean(xf * xf, axis=-1, keepdims=True) + EPS) * g


def _mod_kernel(c_ref, w_ref, b_ref, o_ref):
    c = c_ref[...]
    o_ref[...] = _dot(jax.nn.silu(c), w_ref[...], precision=lax.Precision.HIGHEST) + b_ref[...]


def _adaln(c, w_mod, b_mod):
    B, D = c.shape
    N = w_mod.shape[1]
    tn = 1024
    return pl.pallas_call(
        _mod_kernel,
        out_shape=jax.ShapeDtypeStruct((B, N), F32),
        grid=(N // tn,),
        in_specs=[pl.BlockSpec((B, D), lambda j: (0, 0)),
                  pl.BlockSpec((D, tn), lambda j: (0, j)),
                  pl.BlockSpec((1, tn), lambda j: (0, j))],
        out_specs=pl.BlockSpec((B, tn), lambda j: (0, j)),
        compiler_params=_cparams(("arbitrary",)),
        name="adaln",
    )(c, w_mod, b_mod.reshape(1, N))


def _inproj_kernel(x_ref, sh_ref, sc_ref, g_ref, wxg_ref, wq_ref, wc_ref, wqi_ref, wki_ref, wwi_ref,
                   wmg_ref, wuk_ref, kvg_ref,
                   xr_ref, gr_ref, ql_ref, ct_ref, qi_ref, ki_ref, wi_ref, ga_ref, gb_ref, *, d_rnn, kv_rank, idx_dim):
    x = x_ref[0]
    u = _rms(x, g_ref[...]) * (1.0 + sc_ref[0]) + sh_ref[0]
    ub = u.astype(BF16)
    xg = _dot(ub, wxg_ref[...])
    xr_ref[0] = xg[:, :d_rnn].astype(BF16)
    gr_ref[0] = xg[:, d_rnn:].astype(BF16)
    q = _dot(ub, wq_ref[...]).astype(BF16)
    hd = q.shape[1] // N_HEADS
    for h in range(N_HEADS):
        ql_ref[0, :, h * kv_rank:(h + 1) * kv_rank] = _dot(q[:, h * hd:(h + 1) * hd], wuk_ref[h]).astype(BF16)
    cn = _rms(_dot(ub, wc_ref[...]), kvg_ref[...])
    half = kv_rank // 2
    lo = lax.bitcast_convert_type(cn[:, :half].astype(BF16).astype(F32), U32)
    hi = lax.bitcast_convert_type(cn[:, half:].astype(BF16).astype(F32), U32)
    ct_ref[0] = (lo >> 16) | (hi & jnp.uint32(0xFFFF0000))
    qi = _dot(ub, wqi_ref[...]).astype(BF16)
    for h in range(IDX_HEADS):
        qi_ref[0, h] = qi[:, h * idx_dim:(h + 1) * idx_dim]
    ki_ref[0] = _dot(ub, wki_ref[...])[:, :idx_dim].astype(BF16)
    wi_ref[0] = _dot_nt(wwi_ref[...], ub)
    mg = jax.nn.sigmoid(_dot(ub, wmg_ref[...]))
    dm = mg.shape[1] // 2
    ga_ref[0] = mg[:, :dm].astype(BF16)
    gb_ref[0] = mg[:, dm:].astype(BF16)


def _inproj(x, sh1, sc1, norm1_g, w_in, w_uk, kv_norm_g, dims):
    B, S, D = x.shape
    d_rnn, d_att, kv_rank, idx_dim = dims
    tm = min(512, S)
    o = 0
    wxg = w_in[:, o:o + 2 * d_rnn].astype(BF16); o += 2 * d_rnn
    wq = w_in[:, o:o + d_att].astype(BF16); o += d_att
    wc = w_in[:, o:o + kv_rank].astype(BF16); o += kv_rank
    wqi = w_in[:, o:o + IDX_HEADS * idx_dim].astype(BF16); o += IDX_HEADS * idx_dim
    wki = jnp.pad(w_in[:, o:o + idx_dim], ((0, 0), (0, LANES - idx_dim))).astype(BF16); o += idx_dim
    wwi = w_in[:, o:o + IDX_HEADS].T.astype(BF16); o += IDX_HEADS
    wmg = w_in[:, o:].astype(BF16)
    hd = d_att // N_HEADS
    wuk = jnp.transpose(w_uk, (1, 2, 0)).astype(BF16)
    const2 = lambda b, t: (0, 0)
    const3 = lambda b, t: (0, 0, 0)
    row = lambda b, t: (b, t, 0)
    per_b = lambda b, t: (b, 0, 0)
    out_shapes = (
        jax.ShapeDtypeStruct((B, S, d_rnn), BF16),
        jax.ShapeDtypeStruct((B, S, d_rnn), BF16),
        jax.ShapeDtypeStruct((B, S, N_HEADS * kv_rank), BF16),
        jax.ShapeDtypeStruct((B, S, kv_rank // 2), U32),
        jax.ShapeDtypeStruct((B, IDX_HEADS, S, idx_dim), BF16),
        jax.ShapeDtypeStruct((B, S, idx_dim), BF16),
        jax.ShapeDtypeStruct((B, IDX_HEADS, S), F32),
        jax.ShapeDtypeStruct((B, S, D), BF16),
        jax.ShapeDtypeStruct((B, S, D), BF16),
    )
    out_specs = (
        pl.BlockSpec((1, tm, d_rnn), row),
        pl.BlockSpec((1, tm, d_rnn), row),
        pl.BlockSpec((1, tm, N_HEADS * kv_rank), row),
        pl.BlockSpec((1, tm, kv_rank // 2), row),
        pl.BlockSpec((1, IDX_HEADS, tm, idx_dim), lambda b, t: (b, 0, t, 0)),
        pl.BlockSpec((1, tm, idx_dim), row),
        pl.BlockSpec((1, IDX_HEADS, tm), lambda b, t: (b, 0, t)),
        pl.BlockSpec((1, tm, D), row),
        pl.BlockSpec((1, tm, D), row),
    )
    in_specs = [
        pl.BlockSpec((1, tm, D), row),
        pl.BlockSpec((1, 1, D), per_b),
        pl.BlockSpec((1, 1, D), per_b),
        pl.BlockSpec((1, D), const2),
        pl.BlockSpec(wxg.shape, const2),
        pl.BlockSpec(wq.shape, const2),
        pl.BlockSpec(wc.shape, const2),
        pl.BlockSpec(wqi.shape, const2),
        pl.BlockSpec(wki.shape, const2),
        pl.BlockSpec(wwi.shape, const2),
        pl.BlockSpec(wmg.shape, const2),
        pl.BlockSpec(wuk.shape, const3),
        pl.BlockSpec((1, kv_rank), const2),
    ]
    return pl.pallas_call(
        functools.partial(_inproj_kernel, d_rnn=d_rnn, kv_rank=kv_rank, idx_dim=idx_dim),
        out_shape=out_shapes,
        grid=(B, S // tm),
        in_specs=in_specs,
        out_specs=out_specs,
        compiler_params=_cparams(("parallel", "parallel")),
        name="inproj",
    )(x, sh1[:, None, :], sc1[:, None, :], norm1_g.reshape(1, D), wxg, wq, wc, wqi, wki, wwi, wmg, wuk,
      kv_norm_g.reshape(1, kv_rank))


def _rglru_kernel(xr_ref, gr_ref, cw_ref, cb_ref, wa_ref, ba_ref, wx_ref, bx_ref, lam_ref, o_ref, *, tt):
    seq, c = xr_ref.shape[1], xr_ref.shape[2]
    row = lax.broadcasted_iota(I32, (tt, c), 0)
    row8 = lax.broadcasted_iota(I32, (SUBLANES, c), 0)
    cw = cw_ref[...]
    cb = cb_ref[...]
    wa, wx = wa_ref[0], wx_ref[0]
    ba, bx = ba_ref[...], bx_ref[...]
    decay = -LRU_C * jax.nn.softplus(-lam_ref[...])

    def tile(t, carry):
        tail, h_prev = carry
        t0 = pl.multiple_of(t * tt, tt)
        x = xr_ref[0, pl.ds(t0, tt), :].astype(F32)
        xc = x * cw[CONV_W - 1:CONV_W, :] + cb
        for d in range(1, CONV_W):
            rolled = pltpu.roll(x, d, 0)
            head = jnp.where(row8 < d, pltpu.roll(tail, d, 0), rolled[:SUBLANES])
            xs = jnp.concatenate([head, rolled[SUBLANES:]], axis=0)
            xc = xc + xs * cw[CONV_W - 1 - d:CONV_W - d, :]
        xb = xc.astype(BF16)
        r = jax.nn.sigmoid(_dot(xb, wa) + ba)
        i = jax.nn.sigmoid(_dot(xb, wx) + bx)
        log_a = decay * r
        a = jnp.exp(log_a)
        b = jnp.sqrt(1.0 - a * a) * (i * xc)
        d = 1
        while d < tt:
            a_s = jnp.where(row >= d, pltpu.roll(a, d, 0), 1.0)
            b_s = jnp.where(row >= d, pltpu.roll(b, d, 0), 0.0)
            b = a * b_s + b
            a = a * a_s
            d *= 2
        h = a * h_prev + b
        o_ref[0, pl.ds(t0, tt), :] = (jax.nn.gelu(gr_ref[0, pl.ds(t0, tt), :].astype(F32)) * h).astype(BF16)
        return x[tt - SUBLANES:], h[tt - 1:tt, :]

    lax.fori_loop(0, seq // tt, tile, (jnp.zeros((SUBLANES, c), F32), jnp.zeros((1, c), F32)))


def _rglru(xr, gr, conv_w, conv_b, w_a, b_a, w_x, b_x, lam):
    B, S, C = xr.shape
    nb, bw = w_a.shape[0], w_a.shape[1]
    tt = min(256, S)
    blk = lambda b, n: (b, 0, n)
    vec = lambda b, n: (0, n)
    mat = lambda b, n: (n, 0, 0)
    return pl.pallas_call(
        functools.partial(_rglru_kernel, tt=tt),
        out_shape=jax.ShapeDtypeStruct((B, S, C), BF16),
        grid=(B, nb),
        in_specs=[pl.BlockSpec((1, S, bw), blk), pl.BlockSpec((1, S, bw), blk),
                  pl.BlockSpec((CONV_W, bw), vec), pl.BlockSpec((1, bw), vec),
                  pl.BlockSpec((1, bw, bw), mat), pl.BlockSpec((1, bw), vec),
                  pl.BlockSpec((1, bw, bw), mat), pl.BlockSpec((1, bw), vec),
                  pl.BlockSpec((1, bw), vec)],
        out_specs=pl.BlockSpec((1, S, bw), blk),
        compiler_params=_cparams(("parallel", "parallel")),
        name="rglru",
    )(xr, gr, conv_w, conv_b.reshape(1, C), w_a.astype(BF16), b_a.reshape(1, C), w_x.astype(BF16),
      b_x.reshape(1, C), lam.reshape(1, C))


def _sortable(x):
    b = lax.bitcast_convert_type(x, I32)
    return b ^ ((b >> 31) & jnp.int32(0x7FFFFFFF))


def _indexer_kernel(qi_ref, ki_ref, wi_ref, sc_ref, par_ref, *, topk, idx_scale):
    i = pl.program_id(1)
    n_keys = (i + 1) * Q_TILE
    n_tiles = (n_keys + KEY_TILE - 1) // KEY_TILE
    qi = qi_ref[0].reshape(IDX_HEADS * Q_TILE, qi_ref.shape[-1])
    w = wi_ref[0]
    qpos = i * Q_TILE + lax.broadcasted_iota(I32, (1, Q_TILE), 1)
    key_rows = lax.broadcasted_iota(I32, (KEY_TILE, Q_TILE), 0)

    def score_tile(kt, carry):
        k0 = pl.multiple_of(kt * KEY_TILE, KEY_TILE)
        rel = jnp.maximum(_dot_nt(ki_ref[0, pl.ds(k0, KEY_TILE), :], qi), 0.0)
        s = rel[:, :Q_TILE] * w[0:1, :]
        for h in range(1, IDX_HEADS):
            s = s + rel[:, h * Q_TILE:(h + 1) * Q_TILE] * w[h:h + 1, :]
        s = s * idx_scale
        s = jnp.where(key_rows + k0 <= qpos, s, -jnp.inf)
        sc_ref[0, 0, pl.ds(k0, KEY_TILE), :] = _sortable(s)
        return carry

    lax.fori_loop(0, n_tiles, score_tile, 0)

    n_ctiles = (n_tiles * KEY_TILE + COUNT_TILE - 1) // COUNT_TILE

    @pl.when(n_ctiles * COUNT_TILE > n_tiles * KEY_TILE)
    def _():
        sc_ref[0, 0, pl.ds(pl.multiple_of(n_tiles * KEY_TILE, KEY_TILE), KEY_TILE), :] = jnp.full(
            (KEY_TILE, Q_TILE), INT_MIN, I32)

    rows = lax.broadcasted_iota(I32, (COUNT_TILE, Q_TILE), 0)

    def count(pred):
        def body(c, acc):
            r0 = pl.multiple_of(c * COUNT_TILE, COUNT_TILE)
            x = sc_ref[0, 0, pl.ds(r0, COUNT_TILE), :]
            m = pred(x, r0).astype(I32)
            return acc + jnp.sum(m.reshape(COUNT_TILE // SUBLANES, SUBLANES, Q_TILE), axis=0)
        acc = lax.fori_loop(0, n_ctiles, body, jnp.zeros((SUBLANES, Q_TILE), I32))
        return jnp.sum(acc, axis=0, keepdims=True)

    short = qpos + 1 <= topk
    s_total = sc_ref.shape[2]

    def bit_step(j, state):
        t, n_t = state
        bit = 31 - j
        cand = jnp.where(bit == 31, jnp.zeros_like(t), t + (jnp.int32(1) << jnp.minimum(bit, 30)))
        n_c = count(lambda x, r0: x >= cand)
        ok = n_c >= topk
        return jnp.where(ok, cand, t), jnp.where(ok, n_c, n_t)

    t_init = jnp.full((1, Q_TILE), INT_MIN, I32)
    t_ge, n_ge = lax.fori_loop(0, 32, bit_step, (t_init, jnp.full((1, Q_TILE), s_total, I32)))
    tied = jnp.logical_and(n_ge > topk, jnp.logical_not(short))

    def tie_cut(_):
        need = topk - count(lambda x, r0: x > t_ge)

        def step(j, c):
            bit = jnp.int32(1) << (jnp.int32(s_total.bit_length() - 1) - j)
            cand = c + bit
            below = count(lambda x, r0: jnp.logical_and(x == t_ge, rows + r0 < cand))
            return jnp.where(below < need, cand, c)
        return lax.fori_loop(0, s_total.bit_length(), step, jnp.zeros((1, Q_TILE), I32))

    any_tied = jnp.max(tied.astype(I32)) > 0
    cut = lax.cond(any_tied, tie_cut, lambda _: jnp.zeros((1, Q_TILE), I32), 0)
    tau = jnp.where(tied, t_ge, t_ge - 1)
    cut = jnp.where(tied, cut, -1)
    neg_inf_key = _sortable(jnp.full((1, Q_TILE), -jnp.inf, F32))
    tau = jnp.where(short, neg_inf_key, tau)
    cut = jnp.where(short, -1, cut)
    tie_flag = jnp.zeros((1, Q_TILE), I32) + any_tied.astype(I32)
    par_ref[0, 0] = jnp.concatenate([tau, cut, tie_flag, jnp.zeros((SUBLANES - 3, Q_TILE), I32)], axis=0)


def _indexer(qi, ki, wi, topk, idx_scale, b0, nb):
    _, H, S, di = qi.shape
    nq = S // Q_TILE
    s_pad = ki.shape[1]
    return pl.pallas_call(
        functools.partial(_indexer_kernel, topk=topk, idx_scale=idx_scale),
        out_shape=(jax.ShapeDtypeStruct((nb, nq, s_pad, Q_TILE), I32),
                   jax.ShapeDtypeStruct((nb, nq, SUBLANES, Q_TILE), I32)),
        grid=(nb, nq),
        in_specs=[pl.BlockSpec((1, H, Q_TILE, di), lambda b, i: (b0 + b, 0, i, 0)),
                  pl.BlockSpec((1, s_pad, di), lambda b, i: (b0 + b, 0, 0)),
                  pl.BlockSpec((1, H, Q_TILE), lambda b, i: (b0 + b, 0, i))],
        out_specs=(pl.BlockSpec((1, 1, s_pad, Q_TILE), lambda b, i: (b, i, 0, 0)),
                   pl.BlockSpec((1, 1, SUBLANES, Q_TILE), lambda b, i: (b, i, 0, 0))),
        compiler_params=_cparams(("parallel", "arbitrary")),
        name="indexer",
    )(qi, ki, wi)


def _select_gather(scores, params, table, topk, seq, b0):
    nb, nq, s_pad, _ = scores.shape
    W = table.shape[1]
    info = plsc.get_sparse_core_info()
    nc, ns = info.num_cores, info.num_subcores
    nw = nc * ns
    n_units = nb * nq
    assert nw % nb == 0 and nq % (nw // nb) == 0 and topk % LANES == 0 and Q_TILE % (2 * SC_CHUNK) == 0
    stride = nw // nb
    units_per_w = n_units // nw
    n_groups = Q_TILE // SC_LANES
    rows_per_q = topk // LANES
    mesh = plsc.VectorSubcoreMesh(core_axis_name="c", subcore_axis_name="s")

    @functools.partial(
        pl.kernel, mesh=mesh,
        out_type=(jax.ShapeDtypeStruct((n_units, Q_TILE, topk), I32),
                  jax.ShapeDtypeStruct((n_units * Q_TILE, topk, W), U32)),
        scratch_types=[
            pltpu.VMEM((2, SC_CHUNK, Q_TILE), I32),
            pltpu.VMEM((SUBLANES, Q_TILE), I32),
            pltpu.VMEM((Q_TILE, topk), I32),
            pltpu.VMEM((2, topk, W), U32),
            pltpu.SemaphoreType.DMA((2,)),
            pltpu.SemaphoreType.DMA((2,)),
            pltpu.SemaphoreType.DMA((2,)),
        ],
        compiler_params=pltpu.CompilerParams(needs_layout_passes=False),
        name="select_gather",
    )
    def k(sc_hbm, par_hbm, tab_hbm, sel_hbm, out_hbm, buf_v, par_v, idx_v, rows_v, sem_c, sem_g, sem_w):
        wid = lax.axis_index("s") * nc + lax.axis_index("c")
        lane = lax.iota(I32, SC_LANES)

        def chunk_copy(unit, c, slot):
            return pltpu.make_async_copy(sc_hbm.at[unit, pl.ds(c * SC_CHUNK, SC_CHUNK)], buf_v.at[slot],
                                         sem_c.at[slot])

        def gather_copies(q, slot):
            return [pltpu.make_async_copy(tab_hbm.at[idx_v.at[q, pl.ds(p * LANES, LANES)]],
                                          rows_v.at[slot, pl.ds(p * LANES, LANES)], sem_g.at[slot])
                    for p in range(rows_per_q)]

        def write_copy(unit, q, slot):
            return pltpu.make_async_copy(rows_v.at[slot], out_hbm.at[unit * Q_TILE + q], sem_w.at[slot])

        @pl.loop(0, units_per_w)
        def _(uu):
            b = wid % nb
            i = wid // nb + stride * uu
            unit = b * nq + i
            base = (b0 + b) * seq
            n_chunks = (i + 1) * (Q_TILE // SC_CHUNK)
            chunk_copy(unit, 0, 0).start()
            pltpu.sync_copy(par_hbm.at[unit], par_v)
            fill = jnp.zeros((SC_LANES,), I32) + base

            @pl.loop(0, Q_TILE)
            def _(j):
                for g in range(topk // SC_LANES):
                    idx_v[j, pl.ds(g * SC_LANES, SC_LANES)] = fill + (lane + g * SC_LANES)

            taus = [par_v[0, pl.ds(g * SC_LANES, SC_LANES)] for g in range(n_groups)]
            cuts = [par_v[1, pl.ds(g * SC_LANES, SC_LANES)] for g in range(n_groups)]
            qvec = [lane + g * SC_LANES for g in range(n_groups)]
            tied = jnp.max(par_v[2, pl.ds(0, SC_LANES)]) > 0

            def process(c, slot, cnts):
                def plain_row(r, cnts):
                    gv = jnp.zeros((SC_LANES,), I32) + (c * SC_CHUNK + r + base)
                    ss = [buf_v[slot, r, pl.ds(g * SC_LANES, SC_LANES)] for g in range(n_groups)]
                    out = []
                    for g in range(n_groups):
                        m = ss[g] > taus[g]
                        plsc.store_scatter(idx_v, [qvec[g], cnts[g]], gv, mask=m)
                        out.append(cnts[g] + m.astype(I32))
                    return tuple(out)

                def tie_row(r, cnts):
                    kv = jnp.zeros((SC_LANES,), I32) + (c * SC_CHUNK + r)
                    gv = kv + base
                    ss = [buf_v[slot, r, pl.ds(g * SC_LANES, SC_LANES)] for g in range(n_groups)]
                    out = []
                    for g in range(n_groups):
                        m = (ss[g] > taus[g]) | ((ss[g] == taus[g]) & (kv <= cuts[g]))
                        m = m & (cnts[g] < topk)
                        plsc.store_scatter(idx_v, [qvec[g], cnts[g]], gv, mask=m)
                        out.append(cnts[g] + m.astype(I32))
                    return tuple(out)

                return lax.cond(tied,
                                lambda cn: plsc.parallel_loop(0, SC_CHUNK, unroll=2, carry=cn)(tie_row),
                                lambda cn: plsc.parallel_loop(0, SC_CHUNK, unroll=2, carry=cn)(plain_row), cnts)

            def pair_body(p, cnts):
                c0 = 2 * p
                chunk_copy(unit, c0, 0).wait()
                chunk_copy(unit, c0 + 1, 1).start()
                cnts = process(c0, 0, cnts)
                chunk_copy(unit, c0 + 1, 1).wait()

                @pl.when(c0 + 2 < n_chunks)
                def _():
                    chunk_copy(unit, c0 + 2, 0).start()

                return process(c0 + 1, 1, cnts)

            zero = jnp.zeros((SC_LANES,), I32)
            lax.fori_loop(0, n_chunks // 2, pair_body, tuple(zero for _ in range(n_groups)))
            pltpu.sync_copy(idx_v, sel_hbm.at[unit])

            for cp in gather_copies(0, 0):
                cp.start()

            @pl.loop(0, Q_TILE // 2)
            def _(j):
                for s in range(2):
                    q = 2 * j + s
                    for cp in gather_copies(q, s):
                        cp.wait()
                    write_copy(unit, q, s).start()

                    @pl.when(q > 0)
                    def _():
                        write_copy(unit, q - 1, 1 - s).wait()

                    @pl.when(q + 1 < Q_TILE)
                    def _():
                        for cp in gather_copies(q + 1, 1 - s):
                            cp.start()

            write_copy(unit, Q_TILE - 1, 1).wait()

    sel, rows = k(scores.reshape(n_units, s_pad, Q_TILE), params.reshape(n_units, SUBLANES, Q_TILE), table)
    return sel.reshape(n_units * Q_TILE, topk), rows


def _attn_kernel(cs_ref, sel_ref, ql_ref, sl_ref, o_ref, *, seq, topk, att_scale, row0):
    tq = cs_ref.shape[0]
    t0 = row0 + pl.program_id(0) * tq
    w = cs_ref[...]
    lo = lax.bitcast_convert_type(w << 16, F32).astype(BF16)
    hi = lax.bitcast_convert_type(w & jnp.uint32(0xFFFF0000), F32).astype(BF16)
    g = jnp.concatenate([lo, hi], axis=-1)
    r = g.shape[-1]
    ql = ql_ref[...].reshape(tq, N_HEADS, r)
    logits = jnp.einsum("qhr,qkr->qhk", ql, g, preferred_element_type=F32) * att_scale
    row = t0 + lax.broadcasted_iota(I32, (tq, 1, 1), 0)
    base = (row // seq) * seq
    qpos = row - base
    sel = sel_ref[...].reshape(tq, 1, topk) - base
    logits = logits - sl_ref[...].reshape(1, N_HEADS, 1) * (qpos - sel).astype(F32)
    slot = lax.broadcasted_iota(I32, (1, 1, topk), 2)
    logits = jnp.where(slot < jnp.minimum(qpos + 1, topk), logits, -jnp.inf)
    p = jax.nn.softmax(logits, axis=-1)
    o = jnp.einsum("qhk,qkr->qhr", p.astype(BF16), g, preferred_element_type=F32)
    o_ref[...] = o.reshape(tq, N_HEADS * r).astype(BF16)


def _attn_kernel_inplace(cs_ref, sel_ref, ql_ref, sl_ref, prev_ref, o_ref, **kw):
    del prev_ref
    _attn_kernel(cs_ref, sel_ref, ql_ref, sl_ref, o_ref, **kw)


def _sparse_attn(csel, sel, qlat, seq, topk, att_scale, row0, olat_prev):
    T, _, wh = csel.shape
    tq = 32
    hr = qlat.shape[1]
    blk0 = row0 // tq
    slopes = (2.0 ** (-8.0 * jnp.arange(1, N_HEADS + 1, dtype=F32) / N_HEADS)).reshape(N_HEADS, 1)
    kw = dict(seq=seq, topk=topk, att_scale=att_scale, row0=row0)
    in_specs = [pl.BlockSpec((tq, topk, wh), lambda i: (i, 0, 0)),
                pl.BlockSpec((tq, topk), lambda i: (i, 0)),
                pl.BlockSpec((tq, hr), lambda i: (blk0 + i, 0)),
                pl.BlockSpec((N_HEADS, 1), lambda i: (0, 0))]
    args = [csel, sel, qlat, slopes]
    if olat_prev is None:
        body, aliases = functools.partial(_attn_kernel, **kw), {}
    else:
        body, aliases = functools.partial(_attn_kernel_inplace, **kw), {len(args): 0}
        in_specs.append(pl.BlockSpec(memory_space=pl.ANY))
        args.append(olat_prev)
    return pl.pallas_call(
        body,
        out_shape=jax.ShapeDtypeStruct(qlat.shape, BF16),
        grid=(T // tq,),
        in_specs=in_specs,
        out_specs=pl.BlockSpec((tq, hr), lambda i: (blk0 + i, 0)),
        input_output_aliases=aliases,
        compiler_params=_cparams(("parallel",)),
        name="sparse_attn",
    )(*args)


def _pack_halves(x):
    n = x.shape[1] // 2
    xb = x.astype(BF16).astype(F32)
    return (lax.bitcast_convert_type(xb[:, :n], U32) >> 16) | (lax.bitcast_convert_type(xb[:, n:], U32)
                                                              & jnp.uint32(0xFFFF0000))


def _unpack_lo(w):
    return lax.bitcast_convert_type(w << 16, F32)


def _unpack_hi(w):
    return lax.bitcast_convert_type(w & jnp.uint32(0xFFFF0000), F32)


def _merge_kernel(ol_ref, hg_ref, ga_ref, gb_ref, x_ref, g1_ref, sh_ref, sc_ref, n2_ref, wuv_ref, wao_ref,
                  wro_ref, wo_ref, wrt_ref, brt_ref, h1_ref, xrow_ref, route_ref):
    ol = ol_ref[0]
    r = wuv_ref.shape[1]
    att = jnp.concatenate([_dot(ol[:, h * r:(h + 1) * r], wuv_ref[h]) for h in range(N_HEADS)], axis=-1)
    y_b = _dot(att.astype(BF16), wao_ref[...])
    y_a = _dot(hg_ref[0], wro_ref[...])
    mixin = ga_ref[0].astype(F32) * y_a + gb_ref[0].astype(F32) * y_b
    mix = _dot(mixin.astype(BF16), wo_ref[...])
    h1 = x_ref[0] + g1_ref[0] * mix
    h1_ref[0] = h1
    u2 = _rms(h1, n2_ref[...]) * (1.0 + sc_ref[0]) + sh_ref[0]
    u_hi = u2.astype(BF16)
    u_lo = (u2 - u_hi.astype(F32)).astype(BF16)
    w_hi, w_lo = wrt_ref[0], wrt_ref[1]
    lg = _dot(u_hi, w_hi) + (_dot(u_hi, w_lo) + _dot(u_lo, w_hi)) + brt_ref[...]
    n_exp = N_GROUPS * EXP_PER_GROUP
    lane = lax.broadcasted_iota(I32, lg.shape, 1)
    big = jnp.int32(LANES)
    is_g = jnp.logical_and(lane >= n_exp, lane < n_exp + N_GROUPS)
    gl = jnp.where(is_g, lg, -jnp.inf)
    gmax = jnp.max(gl, axis=-1, keepdims=True)
    g_sel = jnp.min(jnp.where(jnp.logical_and(is_g, gl == gmax), lane, big), axis=-1, keepdims=True) - n_exp
    g_w = 1.0 / jnp.sum(jnp.where(is_g, jnp.exp(gl - gmax), 0.0), axis=-1, keepdims=True)
    in_grp = jnp.logical_and(lane >= g_sel * EXP_PER_GROUP, lane < (g_sel + 1) * EXP_PER_GROUP)
    el = jnp.where(in_grp, lg, -jnp.inf)
    e1 = jnp.max(el, axis=-1, keepdims=True)
    i1 = jnp.min(jnp.where(jnp.logical_and(in_grp, el == e1), lane, big), axis=-1, keepdims=True)
    el2 = jnp.where(lane == i1, -jnp.inf, el)
    e2 = jnp.max(el2, axis=-1, keepdims=True)
    i2 = jnp.min(jnp.where(jnp.logical_and(in_grp, el2 == e2), lane, big), axis=-1, keepdims=True)
    x2 = jnp.exp(e2 - e1)
    den = 1.0 + x2
    gates = jnp.where(lane == i1, g_w * (1.0 / den), 0.0) + jnp.where(lane == i2, g_w * (x2 / den), 0.0)
    gate_words = lax.bitcast_convert_type(gates.astype(BF16).astype(F32), U32) >> 16
    xrow_ref[0] = jnp.concatenate([_pack_halves(u2), gate_words], axis=-1)
    route_ref[0] = jnp.where(lane == 0, i1, jnp.where(lane == 1, i2, 0))


def _merge(olat, hg, ga, gb, x, g1, sh2, sc2, norm2_g, w_uv, w_att_out, w_rnn_out, w_out, w_group, b_group,
           w_expert, b_expert):
    B, S, D = x.shape
    tm = min(512, S)
    n_exp = w_expert.shape[1]
    wuv = jnp.transpose(w_uv, (1, 0, 2)).astype(BF16)
    wrt = jnp.pad(jnp.concatenate([w_expert, w_group], axis=1), ((0, 0), (0, LANES - n_exp - N_GROUPS)))
    wrt_hi = wrt.astype(BF16)
    wrt = jnp.stack([wrt_hi, (wrt - wrt_hi.astype(F32)).astype(BF16)])
    brt = jnp.pad(jnp.concatenate([b_expert, b_group]), (0, LANES - n_exp - N_GROUPS)).reshape(1, LANES)
    row = lambda b, t: (b, t, 0)
    per_b = lambda b, t: (b, 0, 0)
    const2 = lambda b, t: (0, 0)
    const3 = lambda b, t: (0, 0, 0)
    olat = olat.reshape(B, S, -1)
    wao, wro, wo = w_att_out.astype(BF16), w_rnn_out.astype(BF16), w_out.astype(BF16)
    return pl.pallas_call(
        _merge_kernel,
        out_shape=(jax.ShapeDtypeStruct((B, S, D), F32), jax.ShapeDtypeStruct((B, S, D // 2 + LANES), U32),
                   jax.ShapeDtypeStruct((B, S, LANES), I32)),
        grid=(B, S // tm),
        in_specs=[pl.BlockSpec((1, tm, olat.shape[-1]), row), pl.BlockSpec((1, tm, hg.shape[-1]), row),
                  pl.BlockSpec((1, tm, D), row), pl.BlockSpec((1, tm, D), row), pl.BlockSpec((1, tm, D), row),
                  pl.BlockSpec((1, 1, D), per_b), pl.BlockSpec((1, 1, D), per_b), pl.BlockSpec((1, 1, D), per_b),
                  pl.BlockSpec((1, D), const2), pl.BlockSpec(wuv.shape, const3), pl.BlockSpec(wao.shape, const2),
                  pl.BlockSpec(wro.shape, const2), pl.BlockSpec(wo.shape, const2), pl.BlockSpec(wrt.shape, const3),
                  pl.BlockSpec((1, LANES), const2)],
        out_specs=(pl.BlockSpec((1, tm, D), row), pl.BlockSpec((1, tm, D // 2 + LANES), row),
                   pl.BlockSpec((1, tm, LANES), row)),
        compiler_params=_cparams(("parallel", "parallel")),
        name="merge",
    )(olat, hg, ga, gb, x, g1[:, None, :], sh2[:, None, :], sc2[:, None, :], norm2_g.reshape(1, D), wuv, wao, wro,
      wo, wrt, brt)


def _sc_row_gather(table, idx):
    _, W = table.shape
    M = idx.shape[0]
    info = plsc.get_sparse_core_info()
    nc, ns = info.num_cores, info.num_subcores
    nw = nc * ns
    assert M % (nw * SC_WIN * 2) == 0
    n_win = M // (nw * SC_WIN)
    mesh = plsc.VectorSubcoreMesh(core_axis_name="c", subcore_axis_name="s")

    @functools.partial(
        pl.kernel, mesh=mesh,
        out_type=jax.ShapeDtypeStruct((M, W), U32),
        scratch_types=[pltpu.VMEM((2, SC_WIN), I32), pltpu.VMEM((2, SC_WIN, W), U32),
                       pltpu.SemaphoreType.DMA((2,)), pltpu.SemaphoreType.DMA((2,))],
        compiler_params=pltpu.CompilerParams(needs_layout_passes=False),
        name="row_gather",
    )
    def k(tab_hbm, idx_hbm, out_hbm, idx_v, rows_v, sem_g, sem_w):
        wid = lax.axis_index("s") * nc + lax.axis_index("c")

        def win_id(j):
            return j * nw + wid

        def gather(slot):
            return pltpu.make_async_copy(tab_hbm.at[idx_v.at[slot]], rows_v.at[slot], sem_g.at[slot])

        def write(j, slot):
            return pltpu.make_async_copy(rows_v.at[slot], out_hbm.at[pl.ds(win_id(j) * SC_WIN, SC_WIN)],
                                         sem_w.at[slot])

        pltpu.sync_copy(idx_hbm.at[win_id(0)], idx_v.at[0])
        gather(0).start()

        @pl.loop(0, n_win // 2)
        def _(jj):
            for s in range(2):
                j = 2 * jj + s
                gather(s).wait()
                write(j, s).start()

                @pl.when(j > 0)
                def _():
                    write(j - 1, 1 - s).wait()

                @pl.when(j + 1 < n_win)
                def _():
                    pltpu.sync_copy(idx_hbm.at[win_id(j + 1)], idx_v.at[1 - s])
                    gather(1 - s).start()

        write(n_win - 1, 1).wait()

    return k(table, idx.reshape(M // SC_WIN, SC_WIN))


def _sc_row_scatter2(table, dest, m_out):
    T, W = table.shape
    info = plsc.get_sparse_core_info()
    nc, ns = info.num_cores, info.num_subcores
    nw = nc * ns
    assert T % (nw * SC_WIN * 2) == 0
    n_win = T // (nw * SC_WIN)
    mesh = plsc.VectorSubcoreMesh(core_axis_name="c", subcore_axis_name="s")

    @functools.partial(
        pl.kernel, mesh=mesh,
        out_type=jax.ShapeDtypeStruct((m_out, W), U32),
        scratch_types=[pltpu.VMEM((2, 2, SC_WIN), I32), pltpu.VMEM((2, SC_WIN, W), U32),
                       pltpu.SemaphoreType.DMA((2,)), pltpu.SemaphoreType.DMA((2,))],
        compiler_params=pltpu.CompilerParams(needs_layout_passes=False),
        name="row_scatter",
    )
    def k(tab_hbm, dst_hbm, out_hbm, idx_v, rows_v, sem_r, sem_w):
        wid = lax.axis_index("s") * nc + lax.axis_index("c")

        def win_id(j):
            return j * nw + wid

        def read(j, slot):
            return pltpu.make_async_copy(tab_hbm.at[pl.ds(win_id(j) * SC_WIN, SC_WIN)], rows_v.at[slot],
                                         sem_r.at[slot])

        def writes(slot):
            return [pltpu.make_async_copy(rows_v.at[slot], out_hbm.at[idx_v.at[slot, kk]], sem_w.at[slot])
                    for kk in range(2)]

        read(0, 0).start()

        @pl.loop(0, n_win // 2)
        def _(jj):
            for s in range(2):
                j = 2 * jj + s
                for kk in range(2):
                    pltpu.sync_copy(dst_hbm.at[kk, win_id(j)], idx_v.at[s, kk])
                read(j, s).wait()
                for cp in writes(s):
                    cp.start()

                @pl.when(j > 0)
                def _():
                    for cp in writes(1 - s):
                        cp.wait()

                @pl.when(j + 1 < n_win)
                def _():
                    read(j + 1, 1 - s).start()

        for cp in writes(1):
            cp.wait()

    return k(table, dest.reshape(2, T // SC_WIN, SC_WIN))


def _moe_dispatch(e12, n_exp, tmm, granule):
    T = e12.shape[0]
    A = 2 * T
    a_pad = -(-(A + n_exp * tmm) // granule) * granule
    n_tiles = a_pad // tmm
    blk = 512
    assert A % blk == 0
    onehot = (e12.reshape(A)[:, None] == jnp.arange(n_exp, dtype=I32)[None, :]).astype(F32)
    oh3 = onehot.reshape(A // blk, blk, n_exp)
    tri = (jnp.arange(blk)[:, None] >= jnp.arange(blk)[None, :]).astype(F32)
    within = jnp.einsum("ij,bje->bie", tri, oh3)
    blk_tot = within[:, -1, :]
    before = jnp.cumsum(blk_tot, axis=0) - blk_tot
    csum = within + before[:, None, :]
    counts = (before[-1] + blk_tot[-1]).astype(I32)
    padded = (counts + tmm - 1) // tmm * tmm
    pend = jnp.cumsum(padded)
    start = (pend - padded).astype(F32)
    dest = jnp.sum(oh3 * (csum - 1.0 + start[None, None, :]), axis=-1).astype(I32).reshape(T, 2)
    tile_expert = jnp.minimum(jnp.searchsorted(pend, jnp.arange(n_tiles, dtype=I32) * tmm, side="right"),
                              n_exp - 1).astype(I32)
    n_valid = (pend[-1] // tmm).astype(I32).reshape(1)
    return a_pad, dest.T, tile_expert, n_valid


def _moe_grouped_kernel(te_ref, nv_ref, xs_ref, w1_ref, w3_ref, w2_ref, y_ref, w1b_ref, w3b_ref, w2b_ref):
    i = pl.program_id(0)

    @pl.when(jnp.logical_or(i == 0, te_ref[i] != te_ref[jnp.maximum(i - 1, 0)]))
    def _():
        w1b_ref[...] = w1_ref[0].astype(BF16)
        w3b_ref[...] = w3_ref[0].astype(BF16)
        w2b_ref[...] = w2_ref[0].astype(BF16)

    @pl.when(i < nv_ref[0])
    def _():
        w = xs_ref[...]
        half = y_ref.shape[1]
        wu = w[:, :half]
        u = jnp.concatenate([_unpack_lo(wu).astype(BF16), _unpack_hi(wu).astype(BF16)], axis=-1)
        gates = _unpack_lo(w[:, half:])
        lane = lax.broadcasted_iota(I32, gates.shape, 1)
        gcol = jnp.sum(jnp.where(lane == te_ref[i], gates, 0.0), axis=-1, keepdims=True)
        hid = jax.nn.silu(_dot(u, w1b_ref[...])) * _dot(u, w3b_ref[...])
        y_ref[...] = _pack_halves(_dot(hid.astype(BF16), w2b_ref[...]) * gcol)


def _moe_grouped(xs, tile_expert, n_valid, w1, w3, w2, tmm):
    a_pad, W = xs.shape
    n_exp, D, de = w1.shape
    wmap = lambda i, te, nv: (te[i], 0, 0)
    return pl.pallas_call(
        _moe_grouped_kernel,
        out_shape=jax.ShapeDtypeStruct((a_pad, D // 2), U32),
        grid_spec=pltpu.PrefetchScalarGridSpec(
            num_scalar_prefetch=2, grid=(a_pad // tmm,),
            in_specs=[pl.BlockSpec((tmm, W), lambda i, te, nv: (i, 0)),
                      pl.BlockSpec((1, D, de), wmap), pl.BlockSpec((1, D, de), wmap), pl.BlockSpec((1, de, D), wmap)],
            out_specs=pl.BlockSpec((tmm, D // 2), lambda i, te, nv: (i, 0)),
            scratch_shapes=[pltpu.VMEM((D, de), BF16), pltpu.VMEM((D, de), BF16), pltpu.VMEM((de, D), BF16)]),
        compiler_params=_cparams(("arbitrary",)),
        name="moe_grouped",
    )(tile_expert, n_valid, xs, w1, w3, w2)


def _final_kernel(h1_ref, y0_ref, y1_ref, g2_ref, fg_ref, o_ref, *, final_norm):
    w0, w1 = y0_ref[...], y1_ref[...]
    moe = jnp.concatenate([_unpack_lo(w0) + _unpack_lo(w1), _unpack_hi(w0) + _unpack_hi(w1)], axis=-1)
    h2 = h1_ref[0] + g2_ref[0] * moe
    o_ref[0] = _rms(h2, fg_ref[...]) if final_norm else h2


def _final(h1, y01, g2, final_g, final_norm):
    B, S, D = h1.shape
    tm = min(1024, S)
    nt = S // tm
    nblk = B * nt
    return pl.pallas_call(
        functools.partial(_final_kernel, final_norm=final_norm),
        out_shape=jax.ShapeDtypeStruct((B, S, D), F32),
        grid=(B, nt),
        in_specs=[pl.BlockSpec((1, tm, D), lambda b, t: (b, t, 0)),
                  pl.BlockSpec((tm, D // 2), lambda b, t: (b * nt + t, 0)),
                  pl.BlockSpec((tm, D // 2), lambda b, t: (nblk + b * nt + t, 0)),
                  pl.BlockSpec((1, 1, D), lambda b, t: (b, 0, 0)), pl.BlockSpec((1, D), lambda b, t: (0, 0))],
        out_specs=pl.BlockSpec((1, tm, D), lambda b, t: (b, t, 0)),
        compiler_params=_cparams(("parallel", "parallel")),
        name="final",
    )(h1, y01, y01, g2[:, None, :], final_g.reshape(1, D))


def _batch_chunks(n_batch):
    if n_batch < 4 or n_batch % 2:
        return [1] * n_batch
    return [1] + [2] * (n_batch // 2 - 1) + [1]


def kernel(x, c, w_mod, b_mod, norm1_g, w_in, conv_w, conv_b, w_rg_a, b_rg_a, w_rg_x, b_rg_x, lru_lambda, kv_norm_g, w_uk, w_uv, w_rnn_out, w_att_out, w_out, norm2_g, w_group, b_group, w_expert, b_expert, moe_w1, moe_w3, moe_w2, final_g):
    B, S, D = x.shape
    depth = w_mod.shape[0]
    d_rnn = conv_w.shape[2]
    kv_rank, _, head_dim = w_uk.shape[1:]
    d_att = N_HEADS * head_dim
    idx_dim = (w_in.shape[2] - 2 * d_rnn - d_att - kv_rank - IDX_HEADS - 2 * D) // (IDX_HEADS + 1)
    topk = min(TOPK_MAX, S // 4)
    idx_scale = idx_dim ** -0.5 * IDX_HEADS ** -0.5
    att_scale = head_dim ** -0.5
    h = x
    for l in range(depth):
        mod = _adaln(c, w_mod[l], b_mod[l])
        sh1, sc1, g1, sh2, sc2, g2 = jnp.split(mod, 6, axis=-1)
        xr, gr, qlat, ctab, qi, ki, wi, ga, gb = _inproj(h, sh1, sc1, norm1_g[l], w_in[l], w_uk[l], kv_norm_g[l],
                                                        (d_rnn, d_att, kv_rank, idx_dim))
        s_pad = -(-S // COUNT_TILE) * COUNT_TILE
        if s_pad != S:
            ki = jnp.pad(ki, ((0, 0), (0, s_pad - S), (0, 0)))
        table = ctab.reshape(B * S, kv_rank // 2)
        qlat2 = qlat.reshape(B * S, -1)
        olat, b0 = None, 0
        for nb in _batch_chunks(B):
            scores, params = _indexer(qi, ki, wi, topk, idx_scale, b0, nb)
            sel, csel = _select_gather(scores, params, table, topk, S, b0)
            olat = _sparse_attn(csel, sel, qlat2, S, topk, att_scale, b0 * S, olat)
            b0 += nb
        hg = _rglru(xr, gr, conv_w[l], conv_b[l], w_rg_a[l], b_rg_a[l], w_rg_x[l], b_rg_x[l], lru_lambda[l])
        h1, xrow, route = _merge(olat, hg, ga, gb, h, g1, sh2, sc2, norm2_g[l], w_uv[l], w_att_out[l],
                                 w_rnn_out[l], w_out[l], w_group[l], b_group[l], w_expert[l], b_expert[l])
        n_exp = moe_w1.shape[1]
        tmm = MOE_TILE
        a_pad, dest, tile_expert, n_valid = _moe_dispatch(route.reshape(B * S, LANES)[:, :2], n_exp, tmm,
                                                          SC_GATHER_GRANULE)
        xs = _sc_row_scatter2(xrow.reshape(B * S, -1), dest, a_pad)
        yp = _moe_grouped(xs, tile_expert, n_valid, moe_w1[l], moe_w3[l], moe_w2[l], tmm)
        y01 = _sc_row_gather(yp, dest.reshape(2 * B * S))
        h = _final(h1, y01, g2, final_g, final_norm=(l == depth - 1))
    return h
```

```python
import functools

import jax
import jax.numpy as jnp
from jax import lax
from jax.experimental import pallas as pl
from jax.experimental.pallas import tpu as pltpu
from jax.experimental.pallas import tpu_sc as plsc

LRU_C = 8.0
CONV_W = 4
N_HEADS = 8
IDX_HEADS = 8
TOPK_MAX = 256
N_GROUPS = 4
EXP_PER_GROUP = 8
EPS = 1e-6

LANES = 128
SUBLANES = 8
SC_LANES = 16
VMEM_LIMIT = 56 * 1024 * 1024

Q_TILE = 128
KEY_TILE = 256
COUNT_TILE = 512
SC_CHUNK = 64
BUCKET_CAP = 128
SC_WIN = 64
SC_GATHER_GRANULE = 2 * 32 * SC_WIN
MOE_TILE = 512
INT_MIN = -(2 ** 31)

F32 = jnp.float32
BF16 = jnp.bfloat16
I32 = jnp.int32
U32 = jnp.uint32


def _cparams(sem):
    return pltpu.CompilerParams(dimension_semantics=sem, vmem_limit_bytes=VMEM_LIMIT)


def _dot(a, b, **kw):
    return jnp.dot(a, b, preferred_element_type=F32, **kw)


def _dot_nt(a, b):
    return lax.dot_general(a, b, (((1,), (1,)), ((), ())), preferred_element_type=F32)


def _rms(xf, g):
    return xf * lax.rsqrt(jnp.mean(xf * xf, axis=-1, keepdims=True) + EPS) * g


def _mod_kernel(c_ref, w_ref, b_ref, o_ref):
    c = c_ref[...]
    o_ref[...] = _dot(jax.nn.silu(c), w_ref[...], precision=lax.Precision.HIGHEST) + b_ref[...]


def _adaln(c, w_mod, b_mod):
    B, D = c.shape
    N = w_mod.shape[1]
    tn = 1024
    return pl.pallas_call(
        _mod_kernel,
        out_shape=jax.ShapeDtypeStruct((B, N), F32),
        grid=(N // tn,),
        in_specs=[pl.BlockSpec((B, D), lambda j: (0, 0)),
                  pl.BlockSpec((D, tn), lambda j: (0, j)),
                  pl.BlockSpec((1, tn), lambda j: (0, j))],
        out_specs=pl.BlockSpec((B, tn), lambda j: (0, j)),
        compiler_params=_cparams(("arbitrary",)),
        name="adaln",
    )(c, w_mod, b_mod.reshape(1, N))


def _inproj_kernel(x_ref, sh_ref, sc_ref, g_ref, wxg_ref, wq_ref, wc_ref, wqi_ref, wki_ref, wwi_ref,
                   wmg_ref, wuk_ref, kvg_ref,
                   xr_ref, gr_ref, ql_ref, ct_ref, qi_ref, ki_ref, wi_ref, ga_ref, gb_ref, *, d_rnn, kv_rank, idx_dim):
    x = x_ref[0]
    u = _rms(x, g_ref[...]) * (1.0 + sc_ref[0]) + sh_ref[0]
    ub = u.astype(BF16)
    xg = _dot(ub, wxg_ref[...])
    xr_ref[0] = xg[:, :d_rnn].astype(BF16)
    gr_ref[0] = xg[:, d_rnn:].astype(BF16)
    q = _dot(ub, wq_ref[...]).astype(BF16)
    hd = q.shape[1] // N_HEADS
    for h in range(N_HEADS):
        ql_ref[0, :, h * kv_rank:(h + 1) * kv_rank] = _dot(q[:, h * hd:(h + 1) * hd], wuk_ref[h]).astype(BF16)
    cn = _rms(_dot(ub, wc_ref[...]), kvg_ref[...])
    half = kv_rank // 2
    lo = lax.bitcast_convert_type(cn[:, :half].astype(BF16).astype(F32), U32)
    hi = lax.bitcast_convert_type(cn[:, half:].astype(BF16).astype(F32), U32)
    ct_ref[0] = (lo >> 16) | (hi & jnp.uint32(0xFFFF0000))
    qi = _dot(ub, wqi_ref[...]).astype(BF16)
    for h in range(IDX_HEADS):
        qi_ref[0, h] = qi[:, h * idx_dim:(h + 1) * idx_dim]
    ki_ref[0] = _dot(ub, wki_ref[...])[:, :idx_dim].astype(BF16)
    wi_ref[0] = _dot_nt(wwi_ref[...], ub)
    mg = jax.nn.sigmoid(_dot(ub, wmg_ref[...]))
    dm = mg.shape[1] // 2
    ga_ref[0] = mg[:, :dm].astype(BF16)
    gb_ref[0] = mg[:, dm:].astype(BF16)


def _inproj(x, sh1, sc1, norm1_g, w_in, w_uk, kv_norm_g, dims):
    B, S, D = x.shape
    d_rnn, d_att, kv_rank, idx_dim = dims
    tm = min(512, S)
    o = 0
    wxg = w_in[:, o:o + 2 * d_rnn].astype(BF16); o += 2 * d_rnn
    wq = w_in[:, o:o + d_att].astype(BF16); o += d_att
    wc = w_in[:, o:o + kv_rank].astype(BF16); o += kv_rank
    wqi = w_in[:, o:o + IDX_HEADS * idx_dim].astype(BF16); o += IDX_HEADS * idx_dim
    wki = jnp.pad(w_in[:, o:o + idx_dim], ((0, 0), (0, LANES - idx_dim))).astype(BF16); o += idx_dim
    wwi = w_in[:, o:o + IDX_HEADS].T.astype(BF16); o += IDX_HEADS
    wmg = w_in[:, o:].astype(BF16)
    hd = d_att // N_HEADS
    wuk = jnp.transpose(w_uk, (1, 2, 0)).astype(BF16)
    const2 = lambda b, t: (0, 0)
    const3 = lambda b, t: (0, 0, 0)
    row = lambda b, t: (b, t, 0)
    per_b = lambda b, t: (b, 0, 0)
    out_shapes = (
        jax.ShapeDtypeStruct((B, S, d_rnn), BF16),
        jax.ShapeDtypeStruct((B, S, d_rnn), BF16),
        jax.ShapeDtypeStruct((B, S, N_HEADS * kv_rank), BF16),
        jax.ShapeDtypeStruct((B, S, kv_rank // 2), U32),
        jax.ShapeDtypeStruct((B, IDX_HEADS, S, idx_dim), BF16),
        jax.ShapeDtypeStruct((B, S, idx_dim), BF16),
        jax.ShapeDtypeStruct((B, IDX_HEADS, S), F32),
        jax.ShapeDtypeStruct((B, S, D), BF16),
        jax.ShapeDtypeStruct((B, S, D), BF16),
    )
    out_specs = (
        pl.BlockSpec((1, tm, d_rnn), row),
        pl.BlockSpec((1, tm, d_rnn), row),
        pl.BlockSpec((1, tm, N_HEADS * kv_rank), row),
        pl.BlockSpec((1, tm, kv_rank // 2), row),
        pl.BlockSpec((1, IDX_HEADS, tm, idx_dim), lambda b, t: (b, 0, t, 0)),
        pl.BlockSpec((1, tm, idx_dim), row),
        pl.BlockSpec((1, IDX_HEADS, tm), lambda b, t: (b, 0, t)),
        pl.BlockSpec((1, tm, D), row),
        pl.BlockSpec((1, tm, D), row),
    )
    in_specs = [
        pl.BlockSpec((1, tm, D), row),
        pl.BlockSpec((1, 1, D), per_b),
        pl.BlockSpec((1, 1, D), per_b),
        pl.BlockSpec((1, D), const2),
        pl.BlockSpec(wxg.shape, const2),
        pl.BlockSpec(wq.shape, const2),
        pl.BlockSpec(wc.shape, const2),
        pl.BlockSpec(wqi.shape, const2),
        pl.BlockSpec(wki.shape, const2),
        pl.BlockSpec(wwi.shape, const2),
        pl.BlockSpec(wmg.shape, const2),
        pl.BlockSpec(wuk.shape, const3),
        pl.BlockSpec((1, kv_rank), const2),
    ]
    return pl.pallas_call(
        functools.partial(_inproj_kernel, d_rnn=d_rnn, kv_rank=kv_rank, idx_dim=idx_dim),
        out_shape=out_shapes,
        grid=(B, S // tm),
        in_specs=in_specs,
        out_specs=out_specs,
        compiler_params=_cparams(("parallel", "parallel")),
        name="inproj",
    )(x, sh1[:, None, :], sc1[:, None, :], norm1_g.reshape(1, D), wxg, wq, wc, wqi, wki, wwi, wmg, wuk,
      kv_norm_g.reshape(1, kv_rank))


def _rglru_kernel(xr_ref, gr_ref, cw_ref, cb_ref, wa_ref, ba_ref, wx_ref, bx_ref, lam_ref, o_ref, *, tt):
    seq, c = xr_ref.shape[1], xr_ref.shape[2]
    row = lax.broadcasted_iota(I32, (tt, c), 0)
    row8 = lax.broadcasted_iota(I32, (SUBLANES, c), 0)
    cw = cw_ref[...]
    cb = cb_ref[...]
    wa, wx = wa_ref[0], wx_ref[0]
    ba, bx = ba_ref[...], bx_ref[...]
    decay = -LRU_C * jax.nn.softplus(-lam_ref[...])

    def tile(t, carry):
        tail, h_prev = carry
        t0 = pl.multiple_of(t * tt, tt)
        x = xr_ref[0, pl.ds(t0, tt), :].astype(F32)
        xc = x * cw[CONV_W - 1:CONV_W, :] + cb
        for d in range(1, CONV_W):
            rolled = pltpu.roll(x, d, 0)
            head = jnp.where(row8 < d, pltpu.roll(tail, d, 0), rolled[:SUBLANES])
            xs = jnp.concatenate([head, rolled[SUBLANES:]], axis=0)
            xc = xc + xs * cw[CONV_W - 1 - d:CONV_W - d, :]
        xb = xc.astype(BF16)
        r = jax.nn.sigmoid(_dot(xb, wa) + ba)
        i = jax.nn.sigmoid(_dot(xb, wx) + bx)
        log_a = decay * r
        a = jnp.exp(log_a)
        b = jnp.sqrt(1.0 - a * a) * (i * xc)
        d = 1
        while d < tt:
            a_s = jnp.where(row >= d, pltpu.roll(a, d, 0), 1.0)
            b_s = jnp.where(row >= d, pltpu.roll(b, d, 0), 0.0)
            b = a * b_s + b
            a = a * a_s
            d *= 2
        h = a * h_prev + b
        o_ref[0, pl.ds(t0, tt), :] = (jax.nn.gelu(gr_ref[0, pl.ds(t0, tt), :].astype(F32)) * h).astype(BF16)
        return x[tt - SUBLANES:], h[tt - 1:tt, :]

    lax.fori_loop(0, seq // tt, tile, (jnp.zeros((SUBLANES, c), F32), jnp.zeros((1, c), F32)))


def _rglru(xr, gr, conv_w, conv_b, w_a, b_a, w_x, b_x, lam):
    B, S, C = xr.shape
    nb, bw = w_a.shape[0], w_a.shape[1]
    tt = min(256, S)
    blk = lambda b, n: (b, 0, n)
    vec = lambda b, n: (0, n)
    mat = lambda b, n: (n, 0, 0)
    return pl.pallas_call(
        functools.partial(_rglru_kernel, tt=tt),
        out_shape=jax.ShapeDtypeStruct((B, S, C), BF16),
        grid=(B, nb),
        in_specs=[pl.BlockSpec((1, S, bw), blk), pl.BlockSpec((1, S, bw), blk),
                  pl.BlockSpec((CONV_W, bw), vec), pl.BlockSpec((1, bw), vec),
                  pl.BlockSpec((1, bw, bw), mat), pl.BlockSpec((1, bw), vec),
                  pl.BlockSpec((1, bw, bw), mat), pl.BlockSpec((1, bw), vec),
                  pl.BlockSpec((1, bw), vec)],
        out_specs=pl.BlockSpec((1, S, bw), blk),
        compiler_params=_cparams(("parallel", "parallel")),
        name="rglru",
    )(xr, gr, conv_w, conv_b.reshape(1, C), w_a.astype(BF16), b_a.reshape(1, C), w_x.astype(BF16),
      b_x.reshape(1, C), lam.reshape(1, C))


def _sortable(x):
    b = lax.bitcast_convert_type(x, I32)
    return b ^ ((b >> 31) & jnp.int32(0x7FFFFFFF))


def _indexer_kernel(qi_ref, ki_ref, wi_ref, sc_ref, par_ref, *, topk, idx_scale):
    i = pl.program_id(1)
    n_keys = (i + 1) * Q_TILE
    n_tiles = (n_keys + KEY_TILE - 1) // KEY_TILE
    qi = qi_ref[0].reshape(IDX_HEADS * Q_TILE, qi_ref.shape[-1])
    w = wi_ref[0]
    qpos = i * Q_TILE + lax.broadcasted_iota(I32, (1, Q_TILE), 1)
    key_rows = lax.broadcasted_iota(I32, (KEY_TILE, Q_TILE), 0)

    def score_tile(kt, carry):
        k0 = pl.multiple_of(kt * KEY_TILE, KEY_TILE)
        rel = jnp.maximum(_dot_nt(ki_ref[0, pl.ds(k0, KEY_TILE), :], qi), 0.0)
        s = rel[:, :Q_TILE] * w[0:1, :]
        for h in range(1, IDX_HEADS):
            s = s + rel[:, h * Q_TILE:(h + 1) * Q_TILE] * w[h:h + 1, :]
        s = s * idx_scale
        s = jnp.where(key_rows + k0 <= qpos, s, -jnp.inf)
        sc_ref[0, 0, pl.ds(k0, KEY_TILE), :] = _sortable(s)
        return carry

    lax.fori_loop(0, n_tiles, score_tile, 0)

    n_ctiles = (n_tiles * KEY_TILE + COUNT_TILE - 1) // COUNT_TILE

    @pl.when(n_ctiles * COUNT_TILE > n_tiles * KEY_TILE)
    def _():
        sc_ref[0, 0, pl.ds(pl.multiple_of(n_tiles * KEY_TILE, KEY_TILE), KEY_TILE), :] = jnp.full(
            (KEY_TILE, Q_TILE), INT_MIN, I32)

    rows = lax.broadcasted_iota(I32, (COUNT_TILE, Q_TILE), 0)

    def count(pred):
        def body(c, acc):
            r0 = pl.multiple_of(c * COUNT_TILE, COUNT_TILE)
            x = sc_ref[0, 0, pl.ds(r0, COUNT_TILE), :]
            m = pred(x, r0).astype(I32)
            return acc + jnp.sum(m.reshape(COUNT_TILE // SUBLANES, SUBLANES, Q_TILE), axis=0)
        acc = lax.fori_loop(0, n_ctiles, body, jnp.zeros((SUBLANES, Q_TILE), I32))
        return jnp.sum(acc, axis=0, keepdims=True)

    short = qpos + 1 <= topk
    s_total = sc_ref.shape[2]

    def bit_step(j, state):
        t, n_t = state
        bit = 31 - j
        cand = jnp.where(bit == 31, jnp.zeros_like(t), t + (jnp.int32(1) << jnp.minimum(bit, 30)))
        n_c = count(lambda x, r0: x >= cand)
        ok = n_c >= topk
        return jnp.where(ok, cand, t), jnp.where(ok, n_c, n_t)

    t_init = jnp.full((1, Q_TILE), INT_MIN, I32)
    t16, n16 = lax.fori_loop(0, 16, bit_step, (t_init, jnp.full((1, Q_TILE), s_total, I32)))
    hi16 = t16 | jnp.int32(0xFFFF)
    bucket = n16 - count(lambda x, r0: x > hi16)
    too_big = jnp.max(jnp.where(short, 0, bucket)) > BUCKET_CAP
    zeros = jnp.zeros((1, Q_TILE), I32)

    def bucket_mode(_):
        return hi16, t16, zeros + 2

    def exact_mode(_):
        t_ge, n_ge = lax.fori_loop(16, 32, bit_step, (t16, n16))
        tied = jnp.logical_and(n_ge > topk, jnp.logical_not(short))

        def tie_cut(_):
            need = topk - count(lambda x, r0: x > t_ge)

            def step(j, c):
                bit = jnp.int32(1) << (jnp.int32(s_total.bit_length() - 1) - j)
                cand = c + bit
                below = count(lambda x, r0: jnp.logical_and(x == t_ge, rows + r0 < cand))
                return jnp.where(below < need, cand, c)
            return lax.fori_loop(0, s_total.bit_length(), step, zeros)

        any_tied = jnp.max(tied.astype(I32)) > 0
        cut = lax.cond(any_tied, tie_cut, lambda _: zeros, 0)
        return jnp.where(tied, t_ge, t_ge - 1), jnp.where(tied, cut, -1), zeros + any_tied.astype(I32)

    tau, cut, mode = lax.cond(too_big, exact_mode, bucket_mode, 0)
    neg_inf_key = _sortable(jnp.full((1, Q_TILE), -jnp.inf, F32))
    tau = jnp.where(short, neg_inf_key, tau)
    cut = jnp.where(short, jnp.where(mode == 2, jnp.int32(2 ** 31 - 1), -1), cut)
    par_ref[0, 0] = jnp.concatenate([tau, cut, mode, jnp.zeros((SUBLANES - 3, Q_TILE), I32)], axis=0)


def _indexer_kernel_after(qi_ref, ki_ref, wi_ref, after_ref, sc_ref, par_ref, **kw):
    del after_ref
    _indexer_kernel(qi_ref, ki_ref, wi_ref, sc_ref, par_ref, **kw)


def _indexer(qi, ki, wi, topk, idx_scale, b0, nb, after):
    _, H, S, di = qi.shape
    nq = S // Q_TILE
    s_pad = ki.shape[1]
    kw = dict(topk=topk, idx_scale=idx_scale)
    in_specs = [pl.BlockSpec((1, H, Q_TILE, di), lambda b, i: (b0 + b, 0, i, 0)),
                pl.BlockSpec((1, s_pad, di), lambda b, i: (b0 + b, 0, 0)),
                pl.BlockSpec((1, H, Q_TILE), lambda b, i: (b0 + b, 0, i))]
    args = [qi, ki, wi]
    body = functools.partial(_indexer_kernel, **kw)
    if after is not None:
        body = functools.partial(_indexer_kernel_after, **kw)
        in_specs.append(pl.BlockSpec(memory_space=pl.ANY))
        args.append(after)
    return pl.pallas_call(
        body,
        out_shape=(jax.ShapeDtypeStruct((nb, nq, s_pad, Q_TILE), I32),
                   jax.ShapeDtypeStruct((nb, nq, SUBLANES, Q_TILE), I32)),
        grid=(nb, nq),
        in_specs=in_specs,
        out_specs=(pl.BlockSpec((1, 1, s_pad, Q_TILE), lambda b, i: (b, i, 0, 0)),
                   pl.BlockSpec((1, 1, SUBLANES, Q_TILE), lambda b, i: (b, i, 0, 0))),
        compiler_params=_cparams(("parallel", "arbitrary")),
        name="indexer",
    )(*args)


def _select_gather(scores, params, table, topk, seq, b0):
    nb, nq, s_pad, _ = scores.shape
    W = table.shape[1]
    info = plsc.get_sparse_core_info()
    nc, ns = info.num_cores, info.num_subcores
    nw = nc * ns
    n_units = nb * nq
    assert nw % nb == 0 and nq % (nw // nb) == 0 and topk % LANES == 0 and Q_TILE % (2 * SC_CHUNK) == 0
    stride = nw // nb
    units_per_w = n_units // nw
    n_groups = Q_TILE // SC_LANES
    rows_per_q = topk // LANES
    n_win = Q_TILE * rows_per_q
    kbits = (s_pad - 1).bit_length()
    kmask = (1 << kbits) - 1
    assert 16 + kbits <= 31
    mesh = plsc.VectorSubcoreMesh(core_axis_name="c", subcore_axis_name="s")

    @functools.partial(
        pl.kernel, mesh=mesh,
        out_type=(jax.ShapeDtypeStruct((n_units, Q_TILE, topk), I32),
                  jax.ShapeDtypeStruct((n_units * Q_TILE, topk, W), U32)),
        scratch_types=[
            pltpu.VMEM((2, SC_CHUNK, Q_TILE), I32),
            pltpu.VMEM((SUBLANES, Q_TILE), I32),
            pltpu.VMEM((Q_TILE, topk), I32),
            pltpu.VMEM((2, LANES, W), U32),
            pltpu.VMEM((Q_TILE, BUCKET_CAP), I32),
            pltpu.SemaphoreType.DMA((2,)),
            pltpu.SemaphoreType.DMA((2,)),
            pltpu.SemaphoreType.DMA((2,)),
        ],
        compiler_params=pltpu.CompilerParams(needs_layout_passes=False),
        name="select_gather",
    )
    def k(sc_hbm, par_hbm, tab_hbm, sel_hbm, out_hbm, buf_v, par_v, idx_v, rows_v, bpk_v, sem_c, sem_g, sem_w):
        wid = lax.axis_index("s") * nc + lax.axis_index("c")
        lane = lax.iota(I32, SC_LANES)

        def chunk_copy(unit, c, slot):
            return pltpu.make_async_copy(sc_hbm.at[unit, pl.ds(c * SC_CHUNK, SC_CHUNK)], buf_v.at[slot],
                                         sem_c.at[slot])

        def gather_copy(h, slot):
            return pltpu.make_async_copy(
                tab_hbm.at[idx_v.at[h // rows_per_q, pl.ds((h % rows_per_q) * LANES, LANES)]],
                rows_v.at[slot], sem_g.at[slot])

        def write_copy(unit, h, slot):
            return pltpu.make_async_copy(
                rows_v.at[slot],
                out_hbm.at[unit * Q_TILE + h // rows_per_q, pl.ds((h % rows_per_q) * LANES, LANES)],
                sem_w.at[slot])

        @pl.loop(0, units_per_w)
        def _(uu):
            b = wid % nb
            i = wid // nb + stride * uu
            unit = b * nq + i
            base = (b0 + b) * seq
            n_chunks = (i + 1) * (Q_TILE // SC_CHUNK)
            chunk_copy(unit, 0, 0).start()
            pltpu.sync_copy(par_hbm.at[unit], par_v)
            fill = jnp.zeros((SC_LANES,), I32) + base

            @pl.loop(0, Q_TILE)
            def _(j):
                for g in range(topk // SC_LANES):
                    idx_v[j, pl.ds(g * SC_LANES, SC_LANES)] = fill + (lane + g * SC_LANES)

            taus = [par_v[0, pl.ds(g * SC_LANES, SC_LANES)] for g in range(n_groups)]
            cuts = [par_v[1, pl.ds(g * SC_LANES, SC_LANES)] for g in range(n_groups)]
            qvec = [lane + g * SC_LANES for g in range(n_groups)]
            mode = jnp.max(par_v[2, pl.ds(0, SC_LANES)])

            def process(c, slot, cnts):
                def plain_row(r, cnts):
                    gv = jnp.zeros((SC_LANES,), I32) + (c * SC_CHUNK + r + base)
                    ss = [buf_v[slot, r, pl.ds(g * SC_LANES, SC_LANES)] for g in range(n_groups)]
                    out = []
                    for g in range(n_groups):
                        m = ss[g] > taus[g]
                        plsc.store_scatter(idx_v, [qvec[g], cnts[g]], gv, mask=m)
                        out.append(cnts[g] + m.astype(I32))
                    return tuple(out)

                def tie_row(r, cnts):
                    kv = jnp.zeros((SC_LANES,), I32) + (c * SC_CHUNK + r)
                    gv = kv + base
                    ss = [buf_v[slot, r, pl.ds(g * SC_LANES, SC_LANES)] for g in range(n_groups)]
                    out = []
                    for g in range(n_groups):
                        m = (ss[g] > taus[g]) | ((ss[g] == taus[g]) & (kv <= cuts[g]))
                        m = m & (cnts[g] < topk)
                        plsc.store_scatter(idx_v, [qvec[g], cnts[g]], gv, mask=m)
                        out.append(cnts[g] + m.astype(I32))
                    return tuple(out)

                def bucket_row(r, bc):
                    rk = jnp.zeros((SC_LANES,), I32) + (kmask - (c * SC_CHUNK + r))
                    ss = [buf_v[slot, r, pl.ds(g * SC_LANES, SC_LANES)] for g in range(n_groups)]
                    out = []
                    for g in range(n_groups):
                        inb = (ss[g] >= cuts[g]) & (ss[g] <= taus[g]) & (bc[g] < BUCKET_CAP)
                        plsc.store_scatter(bpk_v, [qvec[g], bc[g]], ((ss[g] & 0xFFFF) << kbits) | rk, mask=inb)
                        out.append(bc[g] + inb.astype(I32))
                    return tuple(out)

                def run(body):
                    return lambda cn: plsc.parallel_loop(0, SC_CHUNK, unroll=2, carry=cn)(body)

                def listed(cb):
                    cn = lax.cond(mode == 1, run(tie_row), run(plain_row), tuple(cb[:n_groups]))
                    return tuple(cn) + tuple(cb[n_groups:])

                def bucketed(cb):
                    return tuple(run(plain_row)(tuple(cb[:n_groups]))) + tuple(run(bucket_row)(tuple(cb[n_groups:])))

                return lax.cond(mode == 2, bucketed, listed, cnts)

            def pair_body(p, cnts):
                c0 = 2 * p
                chunk_copy(unit, c0, 0).wait()
                chunk_copy(unit, c0 + 1, 1).start()
                cnts = process(c0, 0, cnts)
                chunk_copy(unit, c0 + 1, 1).wait()

                @pl.when(c0 + 2 < n_chunks)
                def _():
                    chunk_copy(unit, c0 + 2, 0).start()

                return process(c0 + 1, 1, cnts)

            zero = jnp.zeros((SC_LANES,), I32)
            fin = lax.fori_loop(0, n_chunks // 2, pair_body, tuple(zero for _ in range(2 * n_groups)))

            @pl.when(mode == 2)
            def _():
                for g in range(n_groups):
                    cnt, m = fin[g], fin[n_groups + g]
                    need = jnp.minimum(topk - cnt, m)
                    max_m = jnp.max(m)

                    def count_ge(cand):
                        def body(j, acc):
                            jv = jnp.zeros((SC_LANES,), I32) + j
                            v = plsc.load_gather(bpk_v, [qvec[g], jv])
                            return acc + ((v >= cand) & (jv < m)).astype(I32)
                        return lax.fori_loop(0, max_m, body, jnp.zeros((SC_LANES,), I32))

                    def bit_step(bi, v):
                        cand = v + (jnp.int32(1) << (15 + kbits - bi))
                        return jnp.where(count_ge(cand) >= need, cand, v)

                    v_cut = lax.fori_loop(0, 16 + kbits, bit_step, jnp.zeros((SC_LANES,), I32))

                    def pick(j, cn):
                        jv = jnp.zeros((SC_LANES,), I32) + j
                        v = plsc.load_gather(bpk_v, [qvec[g], jv])
                        take = (jv < m) & (need > 0) & (v >= v_cut)
                        plsc.store_scatter(idx_v, [qvec[g], cn], (kmask - (v & kmask)) + base, mask=take)
                        return cn + take.astype(I32)

                    lax.fori_loop(0, max_m, pick, cnt)

            pltpu.sync_copy(idx_v, sel_hbm.at[unit])

            gather_copy(0, 0).start()

            @pl.loop(0, n_win // 2)
            def _(j):
                for s in range(2):
                    h = 2 * j + s
                    gather_copy(h, s).wait()
                    write_copy(unit, h, s).start()

                    @pl.when(h > 0)
                    def _():
                        write_copy(unit, h - 1, 1 - s).wait()

                    @pl.when(h + 1 < n_win)
                    def _():
                        gather_copy(h + 1, 1 - s).start()

            write_copy(unit, n_win - 1, 1).wait()

    sel, rows = k(scores.reshape(n_units, s_pad, Q_TILE), params.reshape(n_units, SUBLANES, Q_TILE), table)
    return sel.reshape(n_units * Q_TILE, topk), rows


def _attn_kernel(cs_ref, sel_ref, ql_ref, sl_ref, o_ref, *, seq, topk, att_scale, row0):
    tq = cs_ref.shape[0]
    t0 = row0 + pl.program_id(0) * tq
    w = cs_ref[...]
    lo = lax.bitcast_convert_type(w << 16, F32).astype(BF16)
    hi = lax.bitcast_convert_type(w & jnp.uint32(0xFFFF0000), F32).astype(BF16)
    g = jnp.concatenate([lo, hi], axis=-1)
    r = g.shape[-1]
    ql = ql_ref[...].reshape(tq, N_HEADS, r)
    logits = jnp.einsum("qhr,qkr->qhk", ql, g, preferred_element_type=F32) * att_scale
    row = t0 + lax.broadcasted_iota(I32, (tq, 1, 1), 0)
    base = (row // seq) * seq
    qpos = row - base
    sel = sel_ref[...].reshape(tq, 1, topk) - base
    logits = logits - sl_ref[...].reshape(1, N_HEADS, 1) * (qpos - sel).astype(F32)
    slot = lax.broadcasted_iota(I32, (1, 1, topk), 2)
    logits = jnp.where(slot < jnp.minimum(qpos + 1, topk), logits, -jnp.inf)
    p = jax.nn.softmax(logits, axis=-1)
    o = jnp.einsum("qhk,qkr->qhr", p.astype(BF16), g, preferred_element_type=F32)
    o_ref[...] = o.reshape(tq, N_HEADS * r).astype(BF16)


def _attn_kernel_inplace(cs_ref, sel_ref, ql_ref, sl_ref, prev_ref, o_ref, **kw):
    del prev_ref
    _attn_kernel(cs_ref, sel_ref, ql_ref, sl_ref, o_ref, **kw)


def _sparse_attn(csel, sel, qlat, seq, topk, att_scale, row0, olat_prev):
    T, _, wh = csel.shape
    tq = 32
    hr = qlat.shape[1]
    blk0 = row0 // tq
    slopes = (2.0 ** (-8.0 * jnp.arange(1, N_HEADS + 1, dtype=F32) / N_HEADS)).reshape(N_HEADS, 1)
    kw = dict(seq=seq, topk=topk, att_scale=att_scale, row0=row0)
    in_specs = [pl.BlockSpec((tq, topk, wh), lambda i: (i, 0, 0)),
                pl.BlockSpec((tq, topk), lambda i: (i, 0)),
                pl.BlockSpec((tq, hr), lambda i: (blk0 + i, 0)),
                pl.BlockSpec((N_HEADS, 1), lambda i: (0, 0))]
    args = [csel, sel, qlat, slopes]
    if olat_prev is None:
        body, aliases = functools.partial(_attn_kernel, **kw), {}
    else:
        body, aliases = functools.partial(_attn_kernel_inplace, **kw), {len(args): 0}
        in_specs.append(pl.BlockSpec(memory_space=pl.ANY))
        args.append(olat_prev)
    return pl.pallas_call(
        body,
        out_shape=jax.ShapeDtypeStruct(qlat.shape, BF16),
        grid=(T // tq,),
        in_specs=in_specs,
        out_specs=pl.BlockSpec((tq, hr), lambda i: (blk0 + i, 0)),
        input_output_aliases=aliases,
        compiler_params=_cparams(("parallel",)),
        name="sparse_attn",
    )(*args)


def _pack_halves(x):
    n = x.shape[1] // 2
    xb = x.astype(BF16).astype(F32)
    return (lax.bitcast_convert_type(xb[:, :n], U32) >> 16) | (lax.bitcast_convert_type(xb[:, n:], U32)
                                                              & jnp.uint32(0xFFFF0000))


def _unpack_lo(w):
    return lax.bitcast_convert_type(w << 16, F32)


def _unpack_hi(w):
    return lax.bitcast_convert_type(w & jnp.uint32(0xFFFF0000), F32)


def _merge_kernel(ol_ref, hg_ref, ga_ref, gb_ref, x_ref, g1_ref, sh_ref, sc_ref, n2_ref, wuv_ref, wao_ref,
                  wro_ref, wo_ref, wrt_ref, brt_ref, h1_ref, xrow_ref, route_ref):
    ol = ol_ref[0]
    r = wuv_ref.shape[1]
    att = jnp.concatenate([_dot(ol[:, h * r:(h + 1) * r], wuv_ref[h]) for h in range(N_HEADS)], axis=-1)
    y_b = _dot(att.astype(BF16), wao_ref[...])
    y_a = _dot(hg_ref[0], wro_ref[...])
    mixin = ga_ref[0].astype(F32) * y_a + gb_ref[0].astype(F32) * y_b
    mix = _dot(mixin.astype(BF16), wo_ref[...])
    h1 = x_ref[0] + g1_ref[0] * mix
    h1_ref[0] = h1
    u2 = _rms(h1, n2_ref[...]) * (1.0 + sc_ref[0]) + sh_ref[0]
    u_hi = u2.astype(BF16)
    u_lo = (u2 - u_hi.astype(F32)).astype(BF16)
    w_hi, w_lo = wrt_ref[0], wrt_ref[1]
    lg = _dot(u_hi, w_hi) + (_dot(u_hi, w_lo) + _dot(u_lo, w_hi)) + brt_ref[...]
    n_exp = N_GROUPS * EXP_PER_GROUP
    lane = lax.broadcasted_iota(I32, lg.shape, 1)
    big = jnp.int32(LANES)
    is_g = jnp.logical_and(lane >= n_exp, lane < n_exp + N_GROUPS)
    gl = jnp.where(is_g, lg, -jnp.inf)
    gmax = jnp.max(gl, axis=-1, keepdims=True)
    g_sel = jnp.min(jnp.where(jnp.logical_and(is_g, gl == gmax), lane, big), axis=-1, keepdims=True) - n_exp
    g_w = 1.0 / jnp.sum(jnp.where(is_g, jnp.exp(gl - gmax), 0.0), axis=-1, keepdims=True)
    in_grp = jnp.logical_and(lane >= g_sel * EXP_PER_GROUP, lane < (g_sel + 1) * EXP_PER_GROUP)
    el = jnp.where(in_grp, lg, -jnp.inf)
    e1 = jnp.max(el, axis=-1, keepdims=True)
    i1 = jnp.min(jnp.where(jnp.logical_and(in_grp, el == e1), lane, big), axis=-1, keepdims=True)
    el2 = jnp.where(lane == i1, -jnp.inf, el)
    e2 = jnp.max(el2, axis=-1, keepdims=True)
    i2 = jnp.min(jnp.where(jnp.logical_and(in_grp, el2 == e2), lane, big), axis=-1, keepdims=True)
    x2 = jnp.exp(e2 - e1)
    den = 1.0 + x2
    gates = jnp.where(lane == i1, g_w * (1.0 / den), 0.0) + jnp.where(lane == i2, g_w * (x2 / den), 0.0)
    gate_words = lax.bitcast_convert_type(gates.astype(BF16).astype(F32), U32) >> 16
    xrow_ref[0] = jnp.concatenate([_pack_halves(u2), gate_words], axis=-1)
    route_ref[0] = jnp.where(lane == 0, i1, jnp.where(lane == 1, i2, 0))


def _merge(olat, hg, ga, gb, x, g1, sh2, sc2, norm2_g, w_uv, w_att_out, w_rnn_out, w_out, w_group, b_group,
           w_expert, b_expert):
    B, S, D = x.shape
    tm = min(512, S)
    n_exp = w_expert.shape[1]
    wuv = jnp.transpose(w_uv, (1, 0, 2)).astype(BF16)
    wrt = jnp.pad(jnp.concatenate([w_expert, w_group], axis=1), ((0, 0), (0, LANES - n_exp - N_GROUPS)))
    wrt_hi = wrt.astype(BF16)
    wrt = jnp.stack([wrt_hi, (wrt - wrt_hi.astype(F32)).astype(BF16)])
    brt = jnp.pad(jnp.concatenate([b_expert, b_group]), (0, LANES - n_exp - N_GROUPS)).reshape(1, LANES)
    row = lambda b, t: (b, t, 0)
    per_b = lambda b, t: (b, 0, 0)
    const2 = lambda b, t: (0, 0)
    const3 = lambda b, t: (0, 0, 0)
    olat = olat.reshape(B, S, -1)
    wao, wro, wo = w_att_out.astype(BF16), w_rnn_out.astype(BF16), w_out.astype(BF16)
    return pl.pallas_call(
        _merge_kernel,
        out_shape=(jax.ShapeDtypeStruct((B, S, D), F32), jax.ShapeDtypeStruct((B, S, D // 2 + LANES), U32),
                   jax.ShapeDtypeStruct((B, S, LANES), I32)),
        grid=(B, S // tm),
        in_specs=[pl.BlockSpec((1, tm, olat.shape[-1]), row), pl.BlockSpec((1, tm, hg.shape[-1]), row),
                  pl.BlockSpec((1, tm, D), row), pl.BlockSpec((1, tm, D), row), pl.BlockSpec((1, tm, D), row),
                  pl.BlockSpec((1, 1, D), per_b), pl.BlockSpec((1, 1, D), per_b), pl.BlockSpec((1, 1, D), per_b),
                  pl.BlockSpec((1, D), const2), pl.BlockSpec(wuv.shape, const3), pl.BlockSpec(wao.shape, const2),
                  pl.BlockSpec(wro.shape, const2), pl.BlockSpec(wo.shape, const2), pl.BlockSpec(wrt.shape, const3),
                  pl.BlockSpec((1, LANES), const2)],
        out_specs=(pl.BlockSpec((1, tm, D), row), pl.BlockSpec((1, tm, D // 2 + LANES), row),
                   pl.BlockSpec((1, tm, LANES), row)),
        compiler_params=_cparams(("parallel", "parallel")),
        name="merge",
    )(olat, hg, ga, gb, x, g1[:, None, :], sh2[:, None, :], sc2[:, None, :], norm2_g.reshape(1, D), wuv, wao, wro,
      wo, wrt, brt)


def _sc_row_gather(table, idx):
    _, W = table.shape
    M = idx.shape[0]
    info = plsc.get_sparse_core_info()
    nc, ns = info.num_cores, info.num_subcores
    nw = nc * ns
    assert M % (nw * SC_WIN * 2) == 0
    n_win = M // (nw * SC_WIN)
    mesh = plsc.VectorSubcoreMesh(core_axis_name="c", subcore_axis_name="s")

    @functools.partial(
        pl.kernel, mesh=mesh,
        out_type=jax.ShapeDtypeStruct((M, W), U32),
        scratch_types=[pltpu.VMEM((2, SC_WIN), I32), pltpu.VMEM((2, SC_WIN, W), U32),
                       pltpu.SemaphoreType.DMA((2,)), pltpu.SemaphoreType.DMA((2,))],
        compiler_params=pltpu.CompilerParams(needs_layout_passes=False),
        name="row_gather",
    )
    def k(tab_hbm, idx_hbm, out_hbm, idx_v, rows_v, sem_g, sem_w):
        wid = lax.axis_index("s") * nc + lax.axis_index("c")

        def win_id(j):
            return j * nw + wid

        def gather(slot):
            return pltpu.make_async_copy(tab_hbm.at[idx_v.at[slot]], rows_v.at[slot], sem_g.at[slot])

        def write(j, slot):
            return pltpu.make_async_copy(rows_v.at[slot], out_hbm.at[pl.ds(win_id(j) * SC_WIN, SC_WIN)],
                                         sem_w.at[slot])

        pltpu.sync_copy(idx_hbm.at[win_id(0)], idx_v.at[0])
        gather(0).start()

        @pl.loop(0, n_win // 2)
        def _(jj):
            for s in range(2):
                j = 2 * jj + s
                gather(s).wait()
                write(j, s).start()

                @pl.when(j > 0)
                def _():
                    write(j - 1, 1 - s).wait()

                @pl.when(j + 1 < n_win)
                def _():
                    pltpu.sync_copy(idx_hbm.at[win_id(j + 1)], idx_v.at[1 - s])
                    gather(1 - s).start()

        write(n_win - 1, 1).wait()

    return k(table, idx.reshape(M // SC_WIN, SC_WIN))


def _sc_row_scatter2(table, dest, m_out):
    T, W = table.shape
    info = plsc.get_sparse_core_info()
    nc, ns = info.num_cores, info.num_subcores
    nw = nc * ns
    assert T % (nw * SC_WIN * 2) == 0
    n_win = T // (nw * SC_WIN)
    mesh = plsc.VectorSubcoreMesh(core_axis_name="c", subcore_axis_name="s")

    @functools.partial(
        pl.kernel, mesh=mesh,
        out_type=jax.ShapeDtypeStruct((m_out, W), U32),
        scratch_types=[pltpu.VMEM((2, 2, SC_WIN), I32), pltpu.VMEM((2, SC_WIN, W), U32),
                       pltpu.SemaphoreType.DMA((2,)), pltpu.SemaphoreType.DMA((2,))],
        compiler_params=pltpu.CompilerParams(needs_layout_passes=False),
        name="row_scatter",
    )
    def k(tab_hbm, dst_hbm, out_hbm, idx_v, rows_v, sem_r, sem_w):
        wid = lax.axis_index("s") * nc + lax.axis_index("c")

        def win_id(j):
            return j * nw + wid

        def read(j, slot):
            return pltpu.make_async_copy(tab_hbm.at[pl.ds(win_id(j) * SC_WIN, SC_WIN)], rows_v.at[slot],
                                         sem_r.at[slot])

        def writes(slot):
            return [pltpu.make_async_copy(rows_v.at[slot], out_hbm.at[idx_v.at[slot, kk]], sem_w.at[slot])
                    for kk in range(2)]

        read(0, 0).start()

        @pl.loop(0, n_win // 2)
        def _(jj):
            for s in range(2):
                j = 2 * jj + s
                for kk in range(2):
                    pltpu.sync_copy(dst_hbm.at[kk, win_id(j)], idx_v.at[s, kk])
                read(j, s).wait()
                for cp in writes(s):
                    cp.start()

                @pl.when(j > 0)
                def _():
                    for cp in writes(1 - s):
                        cp.wait()

                @pl.when(j + 1 < n_win)
                def _():
                    read(j + 1, 1 - s).start()

        for cp in writes(1):
            cp.wait()

    return k(table, dest.reshape(2, T // SC_WIN, SC_WIN))


def _moe_dispatch(e12, n_exp, tmm, granule):
    T = e12.shape[0]
    A = 2 * T
    a_pad = -(-(A + n_exp * tmm) // granule) * granule
    n_tiles = a_pad // tmm
    blk = 512
    assert A % blk == 0
    onehot = (e12.reshape(A)[:, None] == jnp.arange(n_exp, dtype=I32)[None, :]).astype(F32)
    oh3 = onehot.reshape(A // blk, blk, n_exp)
    tri = (jnp.arange(blk)[:, None] >= jnp.arange(blk)[None, :]).astype(F32)
    within = jnp.einsum("ij,bje->bie", tri, oh3)
    blk_tot = within[:, -1, :]
    before = jnp.cumsum(blk_tot, axis=0) - blk_tot
    csum = within + before[:, None, :]
    counts = (before[-1] + blk_tot[-1]).astype(I32)
    padded = (counts + tmm - 1) // tmm * tmm
    pend = jnp.cumsum(padded)
    start = (pend - padded).astype(F32)
    dest = jnp.sum(oh3 * (csum - 1.0 + start[None, None, :]), axis=-1).astype(I32).reshape(T, 2)
    tile_expert = jnp.minimum(jnp.searchsorted(pend, jnp.arange(n_tiles, dtype=I32) * tmm, side="right"),
                              n_exp - 1).astype(I32)
    n_valid = (pend[-1] // tmm).astype(I32).reshape(1)
    return a_pad, dest.T, tile_expert, n_valid


def _moe_grouped_kernel(te_ref, nv_ref, xs_ref, w1_ref, w3_ref, w2_ref, y_ref, w1b_ref, w3b_ref, w2b_ref):
    i = pl.program_id(0)

    @pl.when(jnp.logical_or(i == 0, te_ref[i] != te_ref[jnp.maximum(i - 1, 0)]))
    def _():
        w1b_ref[...] = w1_ref[0].astype(BF16)
        w3b_ref[...] = w3_ref[0].astype(BF16)
        w2b_ref[...] = w2_ref[0].astype(BF16)

    @pl.when(i < nv_ref[0])
    def _():
        w = xs_ref[...]
        half = y_ref.shape[1]
        wu = w[:, :half]
        u = jnp.concatenate([_unpack_lo(wu).astype(BF16), _unpack_hi(wu).astype(BF16)], axis=-1)
        gates = _unpack_lo(w[:, half:])
        lane = lax.broadcasted_iota(I32, gates.shape, 1)
        gcol = jnp.sum(jnp.where(lane == te_ref[i], gates, 0.0), axis=-1, keepdims=True)
        hid = jax.nn.silu(_dot(u, w1b_ref[...])) * _dot(u, w3b_ref[...])
        y_ref[...] = _pack_halves(_dot(hid.astype(BF16), w2b_ref[...]) * gcol)


def _moe_grouped(xs, tile_expert, n_valid, w1, w3, w2, tmm):
    a_pad, W = xs.shape
    n_exp, D, de = w1.shape
    wmap = lambda i, te, nv: (te[i], 0, 0)
    return pl.pallas_call(
        _moe_grouped_kernel,
        out_shape=jax.ShapeDtypeStruct((a_pad, D // 2), U32),
        grid_spec=pltpu.PrefetchScalarGridSpec(
            num_scalar_prefetch=2, grid=(a_pad // tmm,),
            in_specs=[pl.BlockSpec((tmm, W), lambda i, te, nv: (i, 0)),
                      pl.BlockSpec((1, D, de), wmap), pl.BlockSpec((1, D, de), wmap), pl.BlockSpec((1, de, D), wmap)],
            out_specs=pl.BlockSpec((tmm, D // 2), lambda i, te, nv: (i, 0)),
            scratch_shapes=[pltpu.VMEM((D, de), BF16), pltpu.VMEM((D, de), BF16), pltpu.VMEM((de, D), BF16)]),
        compiler_params=_cparams(("arbitrary",)),
        name="moe_grouped",
    )(tile_expert, n_valid, xs, w1, w3, w2)


def _final_kernel(h1_ref, y0_ref, y1_ref, g2_ref, fg_ref, o_ref, *, final_norm):
    w0, w1 = y0_ref[...], y1_ref[...]
    moe = jnp.concatenate([_unpack_lo(w0) + _unpack_lo(w1), _unpack_hi(w0) + _unpack_hi(w1)], axis=-1)
    h2 = h1_ref[0] + g2_ref[0] * moe
    o_ref[0] = _rms(h2, fg_ref[...]) if final_norm else h2


def _final(h1, y01, g2, final_g, final_norm):
    B, S, D = h1.shape
    tm = min(1024, S)
    nt = S // tm
    nblk = B * nt
    return pl.pallas_call(
        functools.partial(_final_kernel, final_norm=final_norm),
        out_shape=jax.ShapeDtypeStruct((B, S, D), F32),
        grid=(B, nt),
        in_specs=[pl.BlockSpec((1, tm, D), lambda b, t: (b, t, 0)),
                  pl.BlockSpec((tm, D // 2), lambda b, t: (b * nt + t, 0)),
                  pl.BlockSpec((tm, D // 2), lambda b, t: (nblk + b * nt + t, 0)),
                  pl.BlockSpec((1, 1, D), lambda b, t: (b, 0, 0)), pl.BlockSpec((1, D), lambda b, t: (0, 0))],
        out_specs=pl.BlockSpec((1, tm, D), lambda b, t: (b, t, 0)),
        compiler_params=_cparams(("parallel", "parallel")),
        name="final",
    )(h1, y01, y01, g2[:, None, :], final_g.reshape(1, D))


def _batch_chunks(n_batch):
    if n_batch < 4 or n_batch % 2:
        return [1] * n_batch
    return [1] + [2] * (n_batch // 2 - 1) + [1]


def kernel(x, c, w_mod, b_mod, norm1_g, w_in, conv_w, conv_b, w_rg_a, b_rg_a, w_rg_x, b_rg_x, lru_lambda, kv_norm_g, w_uk, w_uv, w_rnn_out, w_att_out, w_out, norm2_g, w_group, b_group, w_expert, b_expert, moe_w1, moe_w3, moe_w2, final_g):
    B, S, D = x.shape
    depth = w_mod.shape[0]
    d_rnn = conv_w.shape[2]
    kv_rank, _, head_dim = w_uk.shape[1:]
    d_att = N_HEADS * head_dim
    idx_dim = (w_in.shape[2] - 2 * d_rnn - d_att - kv_rank - IDX_HEADS - 2 * D) // (IDX_HEADS + 1)
    topk = min(TOPK_MAX, S // 4)
    idx_scale = idx_dim ** -0.5 * IDX_HEADS ** -0.5
    att_scale = head_dim ** -0.5
    h = x
    for l in range(depth):
        mod = _adaln(c, w_mod[l], b_mod[l])
        sh1, sc1, g1, sh2, sc2, g2 = jnp.split(mod, 6, axis=-1)
        xr, gr, qlat, ctab, qi, ki, wi, ga, gb = _inproj(h, sh1, sc1, norm1_g[l], w_in[l], w_uk[l], kv_norm_g[l],
                                                        (d_rnn, d_att, kv_rank, idx_dim))
        s_pad = -(-S // COUNT_TILE) * COUNT_TILE
        if s_pad != S:
            ki = jnp.pad(ki, ((0, 0), (0, s_pad - S), (0, 0)))
        table = ctab.reshape(B * S, kv_rank // 2)
        qlat2 = qlat.reshape(B * S, -1)
        olat, b0, waiting, hg = None, 0, [], None
        for step, nb in enumerate(_batch_chunks(B)):
            scores, params = _indexer(qi, ki, wi, topk, idx_scale, b0, nb, olat)
            waiting.append(_select_gather(scores, params, table, topk, S, b0) + (b0,))
            if step == 2:
                hg = _rglru(xr, gr, conv_w[l], conv_b[l], w_rg_a[l], b_rg_a[l], w_rg_x[l], b_rg_x[l], lru_lambda[l])
            if len(waiting) > 1:
                sel, csel, bq = waiting.pop(0)
                olat = _sparse_attn(csel, sel, qlat2, S, topk, att_scale, bq * S, olat)
            b0 += nb
        if hg is None:
            hg = _rglru(xr, gr, conv_w[l], conv_b[l], w_rg_a[l], b_rg_a[l], w_rg_x[l], b_rg_x[l], lru_lambda[l])
        for sel, csel, bq in waiting:
            olat = _sparse_attn(csel, sel, qlat2, S, topk, att_scale, bq * S, olat)
        h1, xrow, route = _merge(olat, hg, ga, gb, h, g1, sh2, sc2, norm2_g[l], w_uv[l], w_att_out[l],
                                 w_rnn_out[l], w_out[l], w_group[l], b_group[l], w_expert[l], b_expert[l])
        n_exp = moe_w1.shape[1]
        tmm = MOE_TILE
        a_pad, dest, tile_expert, n_valid = _moe_dispatch(route.reshape(B * S, LANES)[:, :2], n_exp, tmm,
                                                          SC_GATHER_GRANULE)
        xs = _sc_row_scatter2(xrow.reshape(B * S, -1), dest, a_pad)
        yp = _moe_grouped(xs, tile_expert, n_valid, moe_w1[l], moe_w3[l], moe_w2[l], tmm)
        y01 = _sc_row_gather(yp, dest.reshape(2 * B * S))
        h = _final(h1, y01, g2, final_g, final_norm=(l == depth - 1))
    return h
```

```python
import functools

import jax
import jax.numpy as jnp
from jax import lax
from jax.experimental import pallas as pl
from jax.experimental.pallas import tpu as pltpu
from jax.experimental.pallas import tpu_sc as plsc

LRU_C = 8.0
CONV_W = 4
N_HEADS = 8
IDX_HEADS = 8
TOPK_MAX = 256
N_GROUPS = 4
EXP_PER_GROUP = 8
EPS = 1e-6

LANES = 128
SUBLANES = 8
SC_LANES = 16
VMEM_LIMIT = 56 * 1024 * 1024

Q_TILE = 128
KEY_TILE = 256
COUNT_TILE = 512
SC_CHUNK = 64
BUCKET_CAP = 128
SC_WIN = 64
SC_GATHER_GRANULE = 2 * 32 * SC_WIN
MOE_TILE = 512
INT_MIN = -(2 ** 31)

F32 = jnp.float32
BF16 = jnp.bfloat16
I32 = jnp.int32
U32 = jnp.uint32


def _cparams(sem):
    return pltpu.CompilerParams(dimension_semantics=sem, vmem_limit_bytes=VMEM_LIMIT)


def _dot(a, b, **kw):
    return jnp.dot(a, b, preferred_element_type=F32, **kw)


def _dot_nt(a, b):
    return lax.dot_general(a, b, (((1,), (1,)), ((), ())), preferred_element_type=F32)


def _rms(xf, g):
    return xf * lax.rsqrt(jnp.mean(xf * xf, axis=-1, keepdims=True) + EPS) * g


def _mod_kernel(c_ref, w_ref, b_ref, o_ref):
    c = c_ref[...]
    o_ref[...] = _dot(jax.nn.silu(c), w_ref[...], precision=lax.Precision.HIGHEST) + b_ref[...]


def _adaln(c, w_mod, b_mod):
    B, D = c.shape
    N = w_mod.shape[1]
    tn = 1024
    return pl.pallas_call(
        _mod_kernel,
        out_shape=jax.ShapeDtypeStruct((B, N), F32),
        grid=(N // tn,),
        in_specs=[pl.BlockSpec((B, D), lambda j: (0, 0)),
                  pl.BlockSpec((D, tn), lambda j: (0, j)),
                  pl.BlockSpec((1, tn), lambda j: (0, j))],
        out_specs=pl.BlockSpec((B, tn), lambda j: (0, j)),
        compiler_params=_cparams(("arbitrary",)),
        name="adaln",
    )(c, w_mod, b_mod.reshape(1, N))


def _inproj_kernel(x_ref, sh_ref, sc_ref, g_ref, wxg_ref, wq_ref, wc_ref, wqi_ref, wki_ref, wwi_ref,
                   wmg_ref, wuk_ref, kvg_ref,
                   xr_ref, gr_ref, ql_ref, ct_ref, qi_ref, ki_ref, wi_ref, ga_ref, gb_ref, *, d_rnn, kv_rank, idx_dim):
    x = x_ref[0]
    u = _rms(x, g_ref[...]) * (1.0 + sc_ref[0]) + sh_ref[0]
    ub = u.astype(BF16)
    xg = _dot(ub, wxg_ref[...])
    xr_ref[0] = xg[:, :d_rnn].astype(BF16)
    gr_ref[0] = xg[:, d_rnn:].astype(BF16)
    q = _dot(ub, wq_ref[...]).astype(BF16)
    hd = q.shape[1] // N_HEADS
    for h in range(N_HEADS):
        ql_ref[0, :, h * kv_rank:(h + 1) * kv_rank] = _dot(q[:, h * hd:(h + 1) * hd], wuk_ref[h]).astype(BF16)
    cn = _rms(_dot(ub, wc_ref[...]), kvg_ref[...])
    half = kv_rank // 2
    lo = lax.bitcast_convert_type(cn[:, :half].astype(BF16).astype(F32), U32)
    hi = lax.bitcast_convert_type(cn[:, half:].astype(BF16).astype(F32), U32)
    ct_ref[0] = (lo >> 16) | (hi & jnp.uint32(0xFFFF0000))
    qi = _dot(ub, wqi_ref[...]).astype(BF16)
    for h in range(IDX_HEADS):
        qi_ref[0, h] = qi[:, h * idx_dim:(h + 1) * idx_dim]
    ki_ref[0] = _dot(ub, wki_ref[...])[:, :idx_dim].astype(BF16)
    wi_ref[0] = _dot_nt(wwi_ref[...], ub)
    mg = jax.nn.sigmoid(_dot(ub, wmg_ref[...]))
    dm = mg.shape[1] // 2
    ga_ref[0] = mg[:, :dm].astype(BF16)
    gb_ref[0] = mg[:, dm:].astype(BF16)


def _inproj(x, sh1, sc1, norm1_g, w_in, w_uk, kv_norm_g, dims):
    B, S, D = x.shape
    d_rnn, d_att, kv_rank, idx_dim = dims
    tm = min(512, S)
    o = 0
    wxg = w_in[:, o:o + 2 * d_rnn].astype(BF16); o += 2 * d_rnn
    wq = w_in[:, o:o + d_att].astype(BF16); o += d_att
    wc = w_in[:, o:o + kv_rank].astype(BF16); o += kv_rank
    wqi = w_in[:, o:o + IDX_HEADS * idx_dim].astype(BF16); o += IDX_HEADS * idx_dim
    wki = jnp.pad(w_in[:, o:o + idx_dim], ((0, 0), (0, LANES - idx_dim))).astype(BF16); o += idx_dim
    wwi = w_in[:, o:o + IDX_HEADS].T.astype(BF16); o += IDX_HEADS
    wmg = w_in[:, o:].astype(BF16)
    hd = d_att // N_HEADS
    wuk = jnp.transpose(w_uk, (1, 2, 0)).astype(BF16)
    const2 = lambda b, t: (0, 0)
    const3 = lambda b, t: (0, 0, 0)
    row = lambda b, t: (b, t, 0)
    per_b = lambda b, t: (b, 0, 0)
    out_shapes = (
        jax.ShapeDtypeStruct((B, S, d_rnn), BF16),
        jax.ShapeDtypeStruct((B, S, d_rnn), BF16),
        jax.ShapeDtypeStruct((B, S, N_HEADS * kv_rank), BF16),
        jax.ShapeDtypeStruct((B, S, kv_rank // 2), U32),
        jax.ShapeDtypeStruct((B, IDX_HEADS, S, idx_dim), BF16),
        jax.ShapeDtypeStruct((B, S, idx_dim), BF16),
        jax.ShapeDtypeStruct((B, IDX_HEADS, S), F32),
        jax.ShapeDtypeStruct((B, S, D), BF16),
        jax.ShapeDtypeStruct((B, S, D), BF16),
    )
    out_specs = (
        pl.BlockSpec((1, tm, d_rnn), row),
        pl.BlockSpec((1, tm, d_rnn), row),
        pl.BlockSpec((1, tm, N_HEADS * kv_rank), row),
        pl.BlockSpec((1, tm, kv_rank // 2), row),
        pl.BlockSpec((1, IDX_HEADS, tm, idx_dim), lambda b, t: (b, 0, t, 0)),
        pl.BlockSpec((1, tm, idx_dim), row),
        pl.BlockSpec((1, IDX_HEADS, tm), lambda b, t: (b, 0, t)),
        pl.BlockSpec((1, tm, D), row),
        pl.BlockSpec((1, tm, D), row),
    )
    in_specs = [
        pl.BlockSpec((1, tm, D), row),
        pl.BlockSpec((1, 1, D), per_b),
        pl.BlockSpec((1, 1, D), per_b),
        pl.BlockSpec((1, D), const2),
        pl.BlockSpec(wxg.shape, const2),
        pl.BlockSpec(wq.shape, const2),
        pl.BlockSpec(wc.shape, const2),
        pl.BlockSpec(wqi.shape, const2),
        pl.BlockSpec(wki.shape, const2),
        pl.BlockSpec(wwi.shape, const2),
        pl.BlockSpec(wmg.shape, const2),
        pl.BlockSpec(wuk.shape, const3),
        pl.BlockSpec((1, kv_rank), const2),
    ]
    return pl.pallas_call(
        functools.partial(_inproj_kernel, d_rnn=d_rnn, kv_rank=kv_rank, idx_dim=idx_dim),
        out_shape=out_shapes,
        grid=(B, S // tm),
        in_specs=in_specs,
        out_specs=out_specs,
        compiler_params=_cparams(("parallel", "parallel")),
        name="inproj",
    )(x, sh1[:, None, :], sc1[:, None, :], norm1_g.reshape(1, D), wxg, wq, wc, wqi, wki, wwi, wmg, wuk,
      kv_norm_g.reshape(1, kv_rank))


def _rglru_kernel(xr_ref, gr_ref, cw_ref, cb_ref, wa_ref, ba_ref, wx_ref, bx_ref, lam_ref, o_ref, *, tt):
    seq, c = xr_ref.shape[1], xr_ref.shape[2]
    row = lax.broadcasted_iota(I32, (tt, c), 0)
    row8 = lax.broadcasted_iota(I32, (SUBLANES, c), 0)
    cw = cw_ref[...]
    cb = cb_ref[...]
    wa, wx = wa_ref[0], wx_ref[0]
    ba, bx = ba_ref[...], bx_ref[...]
    decay = -LRU_C * jax.nn.softplus(-lam_ref[...])

    def tile(t, carry):
        tail, h_prev = carry
        t0 = pl.multiple_of(t * tt, tt)
        x = xr_ref[0, pl.ds(t0, tt), :].astype(F32)
        xc = x * cw[CONV_W - 1:CONV_W, :] + cb
        for d in range(1, CONV_W):
            rolled = pltpu.roll(x, d, 0)
            head = jnp.where(row8 < d, pltpu.roll(tail, d, 0), rolled[:SUBLANES])
            xs = jnp.concatenate([head, rolled[SUBLANES:]], axis=0)
            xc = xc + xs * cw[CONV_W - 1 - d:CONV_W - d, :]
        xb = xc.astype(BF16)
        r = jax.nn.sigmoid(_dot(xb, wa) + ba)
        i = jax.nn.sigmoid(_dot(xb, wx) + bx)
        log_a = decay * r
        a = jnp.exp(log_a)
        b = jnp.sqrt(1.0 - a * a) * (i * xc)
        d = 1
        while d < tt:
            a_s = jnp.where(row >= d, pltpu.roll(a, d, 0), 1.0)
            b_s = jnp.where(row >= d, pltpu.roll(b, d, 0), 0.0)
            b = a * b_s + b
            a = a * a_s
            d *= 2
        h = a * h_prev + b
        o_ref[0, pl.ds(t0, tt), :] = (jax.nn.gelu(gr_ref[0, pl.ds(t0, tt), :].astype(F32)) * h).astype(BF16)
        return x[tt - SUBLANES:], h[tt - 1:tt, :]

    lax.fori_loop(0, seq // tt, tile, (jnp.zeros((SUBLANES, c), F32), jnp.zeros((1, c), F32)))


def _rglru_kernel_after(*refs, tt):
    _rglru_kernel(*refs[:9], refs[10], tt=tt)


def _rglru(xr, gr, conv_w, conv_b, w_a, b_a, w_x, b_x, lam, after=None):
    B, S, C = xr.shape
    nb, bw = w_a.shape[0], w_a.shape[1]
    tt = min(256, S)
    blk = lambda b, n: (b, 0, n)
    vec = lambda b, n: (0, n)
    mat = lambda b, n: (n, 0, 0)
    in_specs = [pl.BlockSpec((1, S, bw), blk), pl.BlockSpec((1, S, bw), blk),
                pl.BlockSpec((CONV_W, bw), vec), pl.BlockSpec((1, bw), vec),
                pl.BlockSpec((1, bw, bw), mat), pl.BlockSpec((1, bw), vec),
                pl.BlockSpec((1, bw, bw), mat), pl.BlockSpec((1, bw), vec),
                pl.BlockSpec((1, bw), vec)]
    args = [xr, gr, conv_w, conv_b.reshape(1, C), w_a.astype(BF16), b_a.reshape(1, C), w_x.astype(BF16),
            b_x.reshape(1, C), lam.reshape(1, C)]
    body = functools.partial(_rglru_kernel, tt=tt)
    if after is not None:
        body = functools.partial(_rglru_kernel_after, tt=tt)
        in_specs.append(pl.BlockSpec(memory_space=pl.ANY))
        args.append(after)
    return pl.pallas_call(
        body,
        out_shape=jax.ShapeDtypeStruct((B, S, C), BF16),
        grid=(B, nb),
        in_specs=in_specs,
        out_specs=pl.BlockSpec((1, S, bw), blk),
        compiler_params=_cparams(("parallel", "parallel")),
        name="rglru",
    )(*args)


def _sortable(x):
    b = lax.bitcast_convert_type(x, I32)
    return b ^ ((b >> 31) & jnp.int32(0x7FFFFFFF))


def _indexer_kernel(qi_ref, ki_ref, wi_ref, sc_ref, par_ref, *, topk, idx_scale):
    i = pl.program_id(1)
    n_keys = (i + 1) * Q_TILE
    n_tiles = (n_keys + KEY_TILE - 1) // KEY_TILE
    qi = qi_ref[0].reshape(IDX_HEADS * Q_TILE, qi_ref.shape[-1])
    w = wi_ref[0]
    qpos = i * Q_TILE + lax.broadcasted_iota(I32, (1, Q_TILE), 1)
    key_rows = lax.broadcasted_iota(I32, (KEY_TILE, Q_TILE), 0)

    def score_tile(kt, carry):
        k0 = pl.multiple_of(kt * KEY_TILE, KEY_TILE)
        rel = jnp.maximum(_dot_nt(ki_ref[0, pl.ds(k0, KEY_TILE), :], qi), 0.0)
        s = rel[:, :Q_TILE] * w[0:1, :]
        for h in range(1, IDX_HEADS):
            s = s + rel[:, h * Q_TILE:(h + 1) * Q_TILE] * w[h:h + 1, :]
        s = s * idx_scale
        s = jnp.where(key_rows + k0 <= qpos, s, -jnp.inf)
        sc_ref[0, 0, pl.ds(k0, KEY_TILE), :] = _sortable(s)
        return carry

    lax.fori_loop(0, n_tiles, score_tile, 0)

    n_ctiles = (n_tiles * KEY_TILE + COUNT_TILE - 1) // COUNT_TILE

    @pl.when(n_ctiles * COUNT_TILE > n_tiles * KEY_TILE)
    def _():
        sc_ref[0, 0, pl.ds(pl.multiple_of(n_tiles * KEY_TILE, KEY_TILE), KEY_TILE), :] = jnp.full(
            (KEY_TILE, Q_TILE), INT_MIN, I32)

    rows = lax.broadcasted_iota(I32, (COUNT_TILE, Q_TILE), 0)

    def count(pred):
        def body(c, acc):
            r0 = pl.multiple_of(c * COUNT_TILE, COUNT_TILE)
            x = sc_ref[0, 0, pl.ds(r0, COUNT_TILE), :]
            m = pred(x, r0).astype(I32)
            return acc + jnp.sum(m.reshape(COUNT_TILE // SUBLANES, SUBLANES, Q_TILE), axis=0)
        acc = lax.fori_loop(0, n_ctiles, body, jnp.zeros((SUBLANES, Q_TILE), I32))
        return jnp.sum(acc, axis=0, keepdims=True)

    short = qpos + 1 <= topk
    s_total = sc_ref.shape[2]

    def bit_step(j, state):
        t, n_t = state
        bit = 31 - j
        cand = jnp.where(bit == 31, jnp.zeros_like(t), t + (jnp.int32(1) << jnp.minimum(bit, 30)))
        n_c = count(lambda x, r0: x >= cand)
        ok = n_c >= topk
        return jnp.where(ok, cand, t), jnp.where(ok, n_c, n_t)

    t_init = jnp.full((1, Q_TILE), INT_MIN, I32)
    t16, n16 = lax.fori_loop(0, 16, bit_step, (t_init, jnp.full((1, Q_TILE), s_total, I32)))
    hi16 = t16 | jnp.int32(0xFFFF)
    bucket = n16 - count(lambda x, r0: x > hi16)
    too_big = jnp.max(jnp.where(short, 0, bucket)) > BUCKET_CAP
    zeros = jnp.zeros((1, Q_TILE), I32)

    def bucket_mode(_):
        return hi16, t16, zeros + 2

    def exact_mode(_):
        t_ge, n_ge = lax.fori_loop(16, 32, bit_step, (t16, n16))
        tied = jnp.logical_and(n_ge > topk, jnp.logical_not(short))

        def tie_cut(_):
            need = topk - count(lambda x, r0: x > t_ge)

            def step(j, c):
                bit = jnp.int32(1) << (jnp.int32(s_total.bit_length() - 1) - j)
                cand = c + bit
                below = count(lambda x, r0: jnp.logical_and(x == t_ge, rows + r0 < cand))
                return jnp.where(below < need, cand, c)
            return lax.fori_loop(0, s_total.bit_length(), step, zeros)

        any_tied = jnp.max(tied.astype(I32)) > 0
        cut = lax.cond(any_tied, tie_cut, lambda _: zeros, 0)
        return jnp.where(tied, t_ge, t_ge - 1), jnp.where(tied, cut, -1), zeros + any_tied.astype(I32)

    tau, cut, mode = lax.cond(too_big, exact_mode, bucket_mode, 0)
    neg_inf_key = _sortable(jnp.full((1, Q_TILE), -jnp.inf, F32))
    tau = jnp.where(short, neg_inf_key, tau)
    cut = jnp.where(short, jnp.where(mode == 2, jnp.int32(2 ** 31 - 1), -1), cut)
    par_ref[0, 0] = jnp.concatenate([tau, cut, mode, jnp.zeros((SUBLANES - 3, Q_TILE), I32)], axis=0)


def _indexer_kernel_after(qi_ref, ki_ref, wi_ref, after_ref, sc_ref, par_ref, **kw):
    del after_ref
    _indexer_kernel(qi_ref, ki_ref, wi_ref, sc_ref, par_ref, **kw)


def _indexer(qi, ki, wi, topk, idx_scale, b0, nb, after):
    _, H, S, di = qi.shape
    nq = S // Q_TILE
    s_pad = ki.shape[1]
    kw = dict(topk=topk, idx_scale=idx_scale)
    in_specs = [pl.BlockSpec((1, H, Q_TILE, di), lambda b, i: (b0 + b, 0, i, 0)),
                pl.BlockSpec((1, s_pad, di), lambda b, i: (b0 + b, 0, 0)),
                pl.BlockSpec((1, H, Q_TILE), lambda b, i: (b0 + b, 0, i))]
    args = [qi, ki, wi]
    body = functools.partial(_indexer_kernel, **kw)
    if after is not None:
        body = functools.partial(_indexer_kernel_after, **kw)
        in_specs.append(pl.BlockSpec(memory_space=pl.ANY))
        args.append(after)
    return pl.pallas_call(
        body,
        out_shape=(jax.ShapeDtypeStruct((nb, nq, s_pad, Q_TILE), I32),
                   jax.ShapeDtypeStruct((nb, nq, SUBLANES, Q_TILE), I32)),
        grid=(nb, nq),
        in_specs=in_specs,
        out_specs=(pl.BlockSpec((1, 1, s_pad, Q_TILE), lambda b, i: (b, i, 0, 0)),
                   pl.BlockSpec((1, 1, SUBLANES, Q_TILE), lambda b, i: (b, i, 0, 0))),
        compiler_params=_cparams(("parallel", "arbitrary")),
        name="indexer",
    )(*args)


def _select_gather(scores, params, table, topk, seq, b0):
    nb, nq, s_pad, _ = scores.shape
    W = table.shape[1]
    info = plsc.get_sparse_core_info()
    nc, ns = info.num_cores, info.num_subcores
    nw = nc * ns
    n_units = nb * nq
    assert nw % nb == 0 and nq % (nw // nb) == 0 and topk % LANES == 0 and Q_TILE % (2 * SC_CHUNK) == 0
    stride = nw // nb
    units_per_w = n_units // nw
    n_groups = Q_TILE // SC_LANES
    rows_per_q = topk // LANES
    n_win = Q_TILE * rows_per_q
    kbits = (s_pad - 1).bit_length()
    kmask = (1 << kbits) - 1
    assert 16 + kbits <= 31
    mesh = plsc.VectorSubcoreMesh(core_axis_name="c", subcore_axis_name="s")

    @functools.partial(
        pl.kernel, mesh=mesh,
        out_type=(jax.ShapeDtypeStruct((n_units, Q_TILE, topk), I32),
                  jax.ShapeDtypeStruct((n_units * Q_TILE, topk, W), U32)),
        scratch_types=[
            pltpu.VMEM((2, SC_CHUNK, Q_TILE), I32),
            pltpu.VMEM((SUBLANES, Q_TILE), I32),
            pltpu.VMEM((Q_TILE, topk), I32),
            pltpu.VMEM((2, LANES, W), U32),
            pltpu.VMEM((Q_TILE, BUCKET_CAP), I32),
            pltpu.SemaphoreType.DMA((2,)),
            pltpu.SemaphoreType.DMA((2,)),
            pltpu.SemaphoreType.DMA((2,)),
        ],
        compiler_params=pltpu.CompilerParams(needs_layout_passes=False),
        name="select_gather",
    )
    def k(sc_hbm, par_hbm, tab_hbm, sel_hbm, out_hbm, buf_v, par_v, idx_v, rows_v, bpk_v, sem_c, sem_g, sem_w):
        wid = lax.axis_index("s") * nc + lax.axis_index("c")
        lane = lax.iota(I32, SC_LANES)

        def chunk_copy(unit, c, slot):
            return pltpu.make_async_copy(sc_hbm.at[unit, pl.ds(c * SC_CHUNK, SC_CHUNK)], buf_v.at[slot],
                                         sem_c.at[slot])

        def gather_copy(h, slot):
            return pltpu.make_async_copy(
                tab_hbm.at[idx_v.at[h // rows_per_q, pl.ds((h % rows_per_q) * LANES, LANES)]],
                rows_v.at[slot], sem_g.at[slot])

        def write_copy(unit, h, slot):
            return pltpu.make_async_copy(
                rows_v.at[slot],
                out_hbm.at[unit * Q_TILE + h // rows_per_q, pl.ds((h % rows_per_q) * LANES, LANES)],
                sem_w.at[slot])

        @pl.loop(0, units_per_w)
        def _(uu):
            b = wid % nb
            i = wid // nb + stride * uu
            unit = b * nq + i
            base = (b0 + b) * seq
            n_chunks = (i + 1) * (Q_TILE // SC_CHUNK)
            chunk_copy(unit, 0, 0).start()
            pltpu.sync_copy(par_hbm.at[unit], par_v)
            fill = jnp.zeros((SC_LANES,), I32) + base

            @pl.loop(0, Q_TILE)
            def _(j):
                for g in range(topk // SC_LANES):
                    idx_v[j, pl.ds(g * SC_LANES, SC_LANES)] = fill + (lane + g * SC_LANES)

            taus = [par_v[0, pl.ds(g * SC_LANES, SC_LANES)] for g in range(n_groups)]
            cuts = [par_v[1, pl.ds(g * SC_LANES, SC_LANES)] for g in range(n_groups)]
            qvec = [lane + g * SC_LANES for g in range(n_groups)]
            mode = jnp.max(par_v[2, pl.ds(0, SC_LANES)])

            def process(c, slot, cnts):
                def plain_row(r, cnts):
                    gv = jnp.zeros((SC_LANES,), I32) + (c * SC_CHUNK + r + base)
                    ss = [buf_v[slot, r, pl.ds(g * SC_LANES, SC_LANES)] for g in range(n_groups)]
                    out = []
                    for g in range(n_groups):
                        m = ss[g] > taus[g]
                        plsc.store_scatter(idx_v, [qvec[g], cnts[g]], gv, mask=m)
                        out.append(cnts[g] + m.astype(I32))
                    return tuple(out)

                def tie_row(r, cnts):
                    kv = jnp.zeros((SC_LANES,), I32) + (c * SC_CHUNK + r)
                    gv = kv + base
                    ss = [buf_v[slot, r, pl.ds(g * SC_LANES, SC_LANES)] for g in range(n_groups)]
                    out = []
                    for g in range(n_groups):
                        m = (ss[g] > taus[g]) | ((ss[g] == taus[g]) & (kv <= cuts[g]))
                        m = m & (cnts[g] < topk)
                        plsc.store_scatter(idx_v, [qvec[g], cnts[g]], gv, mask=m)
                        out.append(cnts[g] + m.astype(I32))
                    return tuple(out)

                def bucket_row(r, bc):
                    rk = jnp.zeros((SC_LANES,), I32) + (kmask - (c * SC_CHUNK + r))
                    ss = [buf_v[slot, r, pl.ds(g * SC_LANES, SC_LANES)] for g in range(n_groups)]
                    out = []
                    for g in range(n_groups):
                        inb = (ss[g] >= cuts[g]) & (ss[g] <= taus[g]) & (bc[g] < BUCKET_CAP)
                        plsc.store_scatter(bpk_v, [qvec[g], bc[g]], ((ss[g] & 0xFFFF) << kbits) | rk, mask=inb)
                        out.append(bc[g] + inb.astype(I32))
                    return tuple(out)

                def run(body):
                    return lambda cn: plsc.parallel_loop(0, SC_CHUNK, unroll=2, carry=cn)(body)

                def listed(cb):
                    cn = lax.cond(mode == 1, run(tie_row), run(plain_row), tuple(cb[:n_groups]))
                    return tuple(cn) + tuple(cb[n_groups:])

                def bucketed(cb):
                    return tuple(run(plain_row)(tuple(cb[:n_groups]))) + tuple(run(bucket_row)(tuple(cb[n_groups:])))

                return lax.cond(mode == 2, bucketed, listed, cnts)

            def pair_body(p, cnts):
                c0 = 2 * p
                chunk_copy(unit, c0, 0).wait()
                chunk_copy(unit, c0 + 1, 1).start()
                cnts = process(c0, 0, cnts)
                chunk_copy(unit, c0 + 1, 1).wait()

                @pl.when(c0 + 2 < n_chunks)
                def _():
                    chunk_copy(unit, c0 + 2, 0).start()

                return process(c0 + 1, 1, cnts)

            zero = jnp.zeros((SC_LANES,), I32)
            fin = lax.fori_loop(0, n_chunks // 2, pair_body, tuple(zero for _ in range(2 * n_groups)))

            @pl.when(mode == 2)
            def _():
                for g in range(n_groups):
                    cnt, m = fin[g], fin[n_groups + g]
                    need = jnp.minimum(topk - cnt, m)
                    max_m = jnp.max(m)

                    def count_ge(cand):
                        def body(j, acc):
                            jv = jnp.zeros((SC_LANES,), I32) + j
                            v = plsc.load_gather(bpk_v, [qvec[g], jv])
                            return acc + ((v >= cand) & (jv < m)).astype(I32)
                        return lax.fori_loop(0, max_m, body, jnp.zeros((SC_LANES,), I32))

                    def bit_step(bi, v):
                        cand = v + (jnp.int32(1) << (15 + kbits - bi))
                        return jnp.where(count_ge(cand) >= need, cand, v)

                    v_cut = lax.fori_loop(0, 16 + kbits, bit_step, jnp.zeros((SC_LANES,), I32))

                    def pick(j, cn):
                        jv = jnp.zeros((SC_LANES,), I32) + j
                        v = plsc.load_gather(bpk_v, [qvec[g], jv])
                        take = (jv < m) & (need > 0) & (v >= v_cut)
                        plsc.store_scatter(idx_v, [qvec[g], cn], (kmask - (v & kmask)) + base, mask=take)
                        return cn + take.astype(I32)

                    lax.fori_loop(0, max_m, pick, cnt)

            pltpu.sync_copy(idx_v, sel_hbm.at[unit])

            gather_copy(0, 0).start()

            @pl.loop(0, n_win // 2)
            def _(j):
                for s in range(2):
                    h = 2 * j + s
                    gather_copy(h, s).wait()
                    write_copy(unit, h, s).start()

                    @pl.when(h > 0)
                    def _():
                        write_copy(unit, h - 1, 1 - s).wait()

                    @pl.when(h + 1 < n_win)
                    def _():
                        gather_copy(h + 1, 1 - s).start()

            write_copy(unit, n_win - 1, 1).wait()

    sel, rows = k(scores.reshape(n_units, s_pad, Q_TILE), params.reshape(n_units, SUBLANES, Q_TILE), table)
    return sel.reshape(n_units * Q_TILE, topk), rows


def _attn_kernel(cs_ref, sel_ref, ql_ref, sl_ref, o_ref, *, seq, topk, att_scale, row0):
    tq = cs_ref.shape[0]
    t0 = row0 + pl.program_id(0) * tq
    w = cs_ref[...]
    lo = lax.bitcast_convert_type(w << 16, F32).astype(BF16)
    hi = lax.bitcast_convert_type(w & jnp.uint32(0xFFFF0000), F32).astype(BF16)
    g = jnp.concatenate([lo, hi], axis=-1)
    r = g.shape[-1]
    ql = ql_ref[...].reshape(tq, N_HEADS, r)
    logits = jnp.einsum("qhr,qkr->qhk", ql, g, preferred_element_type=F32) * att_scale
    row = t0 + lax.broadcasted_iota(I32, (tq, 1, 1), 0)
    base = (row // seq) * seq
    qpos = row - base
    sel = sel_ref[...].reshape(tq, 1, topk) - base
    logits = logits - sl_ref[...].reshape(1, N_HEADS, 1) * (qpos - sel).astype(F32)
    slot = lax.broadcasted_iota(I32, (1, 1, topk), 2)
    logits = jnp.where(slot < jnp.minimum(qpos + 1, topk), logits, -jnp.inf)
    p = jax.nn.softmax(logits, axis=-1)
    o = jnp.einsum("qhk,qkr->qhr", p.astype(BF16), g, preferred_element_type=F32)
    o_ref[...] = o.reshape(tq, N_HEADS * r).astype(BF16)


def _attn_kernel_extra(cs_ref, sel_ref, ql_ref, sl_ref, *rest, **kw):
    _attn_kernel(cs_ref, sel_ref, ql_ref, sl_ref, rest[-1], **kw)


def _sparse_attn(csel, sel, qlat, seq, topk, att_scale, row0, olat_prev, after=None):
    T, _, wh = csel.shape
    tq = 32
    hr = qlat.shape[1]
    blk0 = row0 // tq
    slopes = (2.0 ** (-8.0 * jnp.arange(1, N_HEADS + 1, dtype=F32) / N_HEADS)).reshape(N_HEADS, 1)
    kw = dict(seq=seq, topk=topk, att_scale=att_scale, row0=row0)
    in_specs = [pl.BlockSpec((tq, topk, wh), lambda i: (i, 0, 0)),
                pl.BlockSpec((tq, topk), lambda i: (i, 0)),
                pl.BlockSpec((tq, hr), lambda i: (blk0 + i, 0)),
                pl.BlockSpec((N_HEADS, 1), lambda i: (0, 0))]
    args = [csel, sel, qlat, slopes]
    body, aliases = functools.partial(_attn_kernel_extra, **kw), {}
    if olat_prev is not None:
        aliases = {len(args): 0}
        in_specs.append(pl.BlockSpec(memory_space=pl.ANY))
        args.append(olat_prev)
    if after is not None:
        in_specs.append(pl.BlockSpec(memory_space=pl.ANY))
        args.append(after)
    return pl.pallas_call(
        body,
        out_shape=jax.ShapeDtypeStruct(qlat.shape, BF16),
        grid=(T // tq,),
        in_specs=in_specs,
        out_specs=pl.BlockSpec((tq, hr), lambda i: (blk0 + i, 0)),
        input_output_aliases=aliases,
        compiler_params=_cparams(("parallel",)),
        name="sparse_attn",
    )(*args)


def _pack_halves(x):
    n = x.shape[1] // 2
    xb = x.astype(BF16).astype(F32)
    return (lax.bitcast_convert_type(xb[:, :n], U32) >> 16) | (lax.bitcast_convert_type(xb[:, n:], U32)
                                                              & jnp.uint32(0xFFFF0000))


def _unpack_lo(w):
    return lax.bitcast_convert_type(w << 16, F32)


def _unpack_hi(w):
    return lax.bitcast_convert_type(w & jnp.uint32(0xFFFF0000), F32)


def _merge_kernel(ol_ref, hg_ref, ga_ref, gb_ref, x_ref, g1_ref, sh_ref, sc_ref, n2_ref, wuv_ref, wao_ref,
                  wro_ref, wo_ref, wrt_ref, brt_ref, h1_ref, xrow_ref, route_ref):
    ol = ol_ref[0]
    r = wuv_ref.shape[1]
    att = jnp.concatenate([_dot(ol[:, h * r:(h + 1) * r], wuv_ref[h]) for h in range(N_HEADS)], axis=-1)
    y_b = _dot(att.astype(BF16), wao_ref[...])
    y_a = _dot(hg_ref[0], wro_ref[...])
    mixin = ga_ref[0].astype(F32) * y_a + gb_ref[0].astype(F32) * y_b
    mix = _dot(mixin.astype(BF16), wo_ref[...])
    h1 = x_ref[0] + g1_ref[0] * mix
    h1_ref[0] = h1
    u2 = _rms(h1, n2_ref[...]) * (1.0 + sc_ref[0]) + sh_ref[0]
    u_hi = u2.astype(BF16)
    u_lo = (u2 - u_hi.astype(F32)).astype(BF16)
    w_hi, w_lo = wrt_ref[0], wrt_ref[1]
    lg = _dot(u_hi, w_hi) + (_dot(u_hi, w_lo) + _dot(u_lo, w_hi)) + brt_ref[...]
    n_exp = N_GROUPS * EXP_PER_GROUP
    lane = lax.broadcasted_iota(I32, lg.shape, 1)
    big = jnp.int32(LANES)
    is_g = jnp.logical_and(lane >= n_exp, lane < n_exp + N_GROUPS)
    gl = jnp.where(is_g, lg, -jnp.inf)
    gmax = jnp.max(gl, axis=-1, keepdims=True)
    g_sel = jnp.min(jnp.where(jnp.logical_and(is_g, gl == gmax), lane, big), axis=-1, keepdims=True) - n_exp
    g_w = 1.0 / jnp.sum(jnp.where(is_g, jnp.exp(gl - gmax), 0.0), axis=-1, keepdims=True)
    in_grp = jnp.logical_and(lane >= g_sel * EXP_PER_GROUP, lane < (g_sel + 1) * EXP_PER_GROUP)
    el = jnp.where(in_grp, lg, -jnp.inf)
    e1 = jnp.max(el, axis=-1, keepdims=True)
    i1 = jnp.min(jnp.where(jnp.logical_and(in_grp, el == e1), lane, big), axis=-1, keepdims=True)
    el2 = jnp.where(lane == i1, -jnp.inf, el)
    e2 = jnp.max(el2, axis=-1, keepdims=True)
    i2 = jnp.min(jnp.where(jnp.logical_and(in_grp, el2 == e2), lane, big), axis=-1, keepdims=True)
    x2 = jnp.exp(e2 - e1)
    den = 1.0 + x2
    gates = jnp.where(lane == i1, g_w * (1.0 / den), 0.0) + jnp.where(lane == i2, g_w * (x2 / den), 0.0)
    gate_words = lax.bitcast_convert_type(gates.astype(BF16).astype(F32), U32) >> 16
    xrow_ref[0] = jnp.concatenate([_pack_halves(u2), gate_words], axis=-1)
    route_ref[0] = jnp.where(lane == 0, i1, jnp.where(lane == 1, i2, 0))


def _merge(olat, hg, ga, gb, x, g1, sh2, sc2, norm2_g, w_uv, w_att_out, w_rnn_out, w_out, w_group, b_group,
           w_expert, b_expert):
    B, S, D = x.shape
    tm = min(512, S)
    n_exp = w_expert.shape[1]
    wuv = jnp.transpose(w_uv, (1, 0, 2)).astype(BF16)
    wrt = jnp.pad(jnp.concatenate([w_expert, w_group], axis=1), ((0, 0), (0, LANES - n_exp - N_GROUPS)))
    wrt_hi = wrt.astype(BF16)
    wrt = jnp.stack([wrt_hi, (wrt - wrt_hi.astype(F32)).astype(BF16)])
    brt = jnp.pad(jnp.concatenate([b_expert, b_group]), (0, LANES - n_exp - N_GROUPS)).reshape(1, LANES)
    row = lambda b, t: (b, t, 0)
    per_b = lambda b, t: (b, 0, 0)
    const2 = lambda b, t: (0, 0)
    const3 = lambda b, t: (0, 0, 0)
    olat = olat.reshape(B, S, -1)
    wao, wro, wo = w_att_out.astype(BF16), w_rnn_out.astype(BF16), w_out.astype(BF16)
    return pl.pallas_call(
        _merge_kernel,
        out_shape=(jax.ShapeDtypeStruct((B, S, D), F32), jax.ShapeDtypeStruct((B, S, D // 2 + LANES), U32),
                   jax.ShapeDtypeStruct((B, S, LANES), I32)),
        grid=(B, S // tm),
        in_specs=[pl.BlockSpec((1, tm, olat.shape[-1]), row), pl.BlockSpec((1, tm, hg.shape[-1]), row),
                  pl.BlockSpec((1, tm, D), row), pl.BlockSpec((1, tm, D), row), pl.BlockSpec((1, tm, D), row),
                  pl.BlockSpec((1, 1, D), per_b), pl.BlockSpec((1, 1, D), per_b), pl.BlockSpec((1, 1, D), per_b),
                  pl.BlockSpec((1, D), const2), pl.BlockSpec(wuv.shape, const3), pl.BlockSpec(wao.shape, const2),
                  pl.BlockSpec(wro.shape, const2), pl.BlockSpec(wo.shape, const2), pl.BlockSpec(wrt.shape, const3),
                  pl.BlockSpec((1, LANES), const2)],
        out_specs=(pl.BlockSpec((1, tm, D), row), pl.BlockSpec((1, tm, D // 2 + LANES), row),
                   pl.BlockSpec((1, tm, LANES), row)),
        compiler_params=_cparams(("parallel", "parallel")),
        name="merge",
    )(olat, hg, ga, gb, x, g1[:, None, :], sh2[:, None, :], sc2[:, None, :], norm2_g.reshape(1, D), wuv, wao, wro,
      wo, wrt, brt)


def _sc_row_gather(table, idx):
    _, W = table.shape
    M = idx.shape[0]
    info = plsc.get_sparse_core_info()
    nc, ns = info.num_cores, info.num_subcores
    nw = nc * ns
    assert M % (nw * SC_WIN * 2) == 0
    n_win = M // (nw * SC_WIN)
    mesh = plsc.VectorSubcoreMesh(core_axis_name="c", subcore_axis_name="s")

    @functools.partial(
        pl.kernel, mesh=mesh,
        out_type=jax.ShapeDtypeStruct((M, W), U32),
        scratch_types=[pltpu.VMEM((2, SC_WIN), I32), pltpu.VMEM((2, SC_WIN, W), U32),
                       pltpu.SemaphoreType.DMA((2,)), pltpu.SemaphoreType.DMA((2,))],
        compiler_params=pltpu.CompilerParams(needs_layout_passes=False),
        name="row_gather",
    )
    def k(tab_hbm, idx_hbm, out_hbm, idx_v, rows_v, sem_g, sem_w):
        wid = lax.axis_index("s") * nc + lax.axis_index("c")

        def win_id(j):
            return j * nw + wid

        def gather(slot):
            return pltpu.make_async_copy(tab_hbm.at[idx_v.at[slot]], rows_v.at[slot], sem_g.at[slot])

        def write(j, slot):
            return pltpu.make_async_copy(rows_v.at[slot], out_hbm.at[pl.ds(win_id(j) * SC_WIN, SC_WIN)],
                                         sem_w.at[slot])

        pltpu.sync_copy(idx_hbm.at[win_id(0)], idx_v.at[0])
        gather(0).start()

        @pl.loop(0, n_win // 2)
        def _(jj):
            for s in range(2):
                j = 2 * jj + s
                gather(s).wait()
                write(j, s).start()

                @pl.when(j > 0)
                def _():
                    write(j - 1, 1 - s).wait()

                @pl.when(j + 1 < n_win)
                def _():
                    pltpu.sync_copy(idx_hbm.at[win_id(j + 1)], idx_v.at[1 - s])
                    gather(1 - s).start()

        write(n_win - 1, 1).wait()

    return k(table, idx.reshape(M // SC_WIN, SC_WIN))


def _sc_row_scatter2(table, dest, m_out):
    T, W = table.shape
    info = plsc.get_sparse_core_info()
    nc, ns = info.num_cores, info.num_subcores
    nw = nc * ns
    assert T % (nw * SC_WIN * 2) == 0
    n_win = T // (nw * SC_WIN)
    mesh = plsc.VectorSubcoreMesh(core_axis_name="c", subcore_axis_name="s")

    @functools.partial(
        pl.kernel, mesh=mesh,
        out_type=jax.ShapeDtypeStruct((m_out, W), U32),
        scratch_types=[pltpu.VMEM((2, 2, SC_WIN), I32), pltpu.VMEM((2, SC_WIN, W), U32),
                       pltpu.SemaphoreType.DMA((2,)), pltpu.SemaphoreType.DMA((2,))],
        compiler_params=pltpu.CompilerParams(needs_layout_passes=False),
        name="row_scatter",
    )
    def k(tab_hbm, dst_hbm, out_hbm, idx_v, rows_v, sem_r, sem_w):
        wid = lax.axis_index("s") * nc + lax.axis_index("c")

        def win_id(j):
            return j * nw + wid

        def read(j, slot):
            return pltpu.make_async_copy(tab_hbm.at[pl.ds(win_id(j) * SC_WIN, SC_WIN)], rows_v.at[slot],
                                         sem_r.at[slot])

        def writes(slot):
            return [pltpu.make_async_copy(rows_v.at[slot], out_hbm.at[idx_v.at[slot, kk]], sem_w.at[slot])
                    for kk in range(2)]

        read(0, 0).start()

        @pl.loop(0, n_win // 2)
        def _(jj):
            for s in range(2):
                j = 2 * jj + s
                for kk in range(2):
                    pltpu.sync_copy(dst_hbm.at[kk, win_id(j)], idx_v.at[s, kk])
                read(j, s).wait()
                for cp in writes(s):
                    cp.start()

                @pl.when(j > 0)
                def _():
                    for cp in writes(1 - s):
                        cp.wait()

                @pl.when(j + 1 < n_win)
                def _():
                    read(j + 1, 1 - s).start()

        for cp in writes(1):
            cp.wait()

    return k(table, dest.reshape(2, T // SC_WIN, SC_WIN))


def _moe_dispatch(e12, n_exp, tmm, granule):
    T = e12.shape[0]
    A = 2 * T
    a_pad = -(-(A + n_exp * tmm) // granule) * granule
    n_tiles = a_pad // tmm
    blk = 512
    assert A % blk == 0
    onehot = (e12.reshape(A)[:, None] == jnp.arange(n_exp, dtype=I32)[None, :]).astype(F32)
    oh3 = onehot.reshape(A // blk, blk, n_exp)
    tri = (jnp.arange(blk)[:, None] >= jnp.arange(blk)[None, :]).astype(F32)
    within = jnp.einsum("ij,bje->bie", tri, oh3)
    blk_tot = within[:, -1, :]
    before = jnp.cumsum(blk_tot, axis=0) - blk_tot
    csum = within + before[:, None, :]
    counts = (before[-1] + blk_tot[-1]).astype(I32)
    padded = (counts + tmm - 1) // tmm * tmm
    pend = jnp.cumsum(padded)
    start = (pend - padded).astype(F32)
    dest = jnp.sum(oh3 * (csum - 1.0 + start[None, None, :]), axis=-1).astype(I32).reshape(T, 2)
    tile_expert = jnp.minimum(jnp.searchsorted(pend, jnp.arange(n_tiles, dtype=I32) * tmm, side="right"),
                              n_exp - 1).astype(I32)
    n_valid = (pend[-1] // tmm).astype(I32).reshape(1)
    return a_pad, dest.T, tile_expert, n_valid


def _moe_grouped_kernel(te_ref, nv_ref, xs_ref, w1_ref, w3_ref, w2_ref, y_ref, w1b_ref, w3b_ref, w2b_ref):
    i = pl.program_id(0)

    @pl.when(jnp.logical_or(i == 0, te_ref[i] != te_ref[jnp.maximum(i - 1, 0)]))
    def _():
        w1b_ref[...] = w1_ref[0].astype(BF16)
        w3b_ref[...] = w3_ref[0].astype(BF16)
        w2b_ref[...] = w2_ref[0].astype(BF16)

    @pl.when(i < nv_ref[0])
    def _():
        w = xs_ref[...]
        half = y_ref.shape[1]
        wu = w[:, :half]
        u = jnp.concatenate([_unpack_lo(wu).astype(BF16), _unpack_hi(wu).astype(BF16)], axis=-1)
        gates = _unpack_lo(w[:, half:])
        lane = lax.broadcasted_iota(I32, gates.shape, 1)
        gcol = jnp.sum(jnp.where(lane == te_ref[i], gates, 0.0), axis=-1, keepdims=True)
        hid = jax.nn.silu(_dot(u, w1b_ref[...])) * _dot(u, w3b_ref[...])
        y_ref[...] = _pack_halves(_dot(hid.astype(BF16), w2b_ref[...]) * gcol)


def _moe_grouped(xs, tile_expert, n_valid, w1, w3, w2, tmm):
    a_pad, W = xs.shape
    n_exp, D, de = w1.shape
    wmap = lambda i, te, nv: (te[i], 0, 0)
    return pl.pallas_call(
        _moe_grouped_kernel,
        out_shape=jax.ShapeDtypeStruct((a_pad, D // 2), U32),
        grid_spec=pltpu.PrefetchScalarGridSpec(
            num_scalar_prefetch=2, grid=(a_pad // tmm,),
            in_specs=[pl.BlockSpec((tmm, W), lambda i, te, nv: (i, 0)),
                      pl.BlockSpec((1, D, de), wmap), pl.BlockSpec((1, D, de), wmap), pl.BlockSpec((1, de, D), wmap)],
            out_specs=pl.BlockSpec((tmm, D // 2), lambda i, te, nv: (i, 0)),
            scratch_shapes=[pltpu.VMEM((D, de), BF16), pltpu.VMEM((D, de), BF16), pltpu.VMEM((de, D), BF16)]),
        compiler_params=_cparams(("arbitrary",)),
        name="moe_grouped",
    )(tile_expert, n_valid, xs, w1, w3, w2)


def _final_kernel(h1_ref, y0_ref, y1_ref, g2_ref, fg_ref, o_ref, *, final_norm):
    w0, w1 = y0_ref[...], y1_ref[...]
    moe = jnp.concatenate([_unpack_lo(w0) + _unpack_lo(w1), _unpack_hi(w0) + _unpack_hi(w1)], axis=-1)
    h2 = h1_ref[0] + g2_ref[0] * moe
    o_ref[0] = _rms(h2, fg_ref[...]) if final_norm else h2


def _final(h1, y01, g2, final_g, final_norm):
    B, S, D = h1.shape
    tm = min(1024, S)
    nt = S // tm
    nblk = B * nt
    return pl.pallas_call(
        functools.partial(_final_kernel, final_norm=final_norm),
        out_shape=jax.ShapeDtypeStruct((B, S, D), F32),
        grid=(B, nt),
        in_specs=[pl.BlockSpec((1, tm, D), lambda b, t: (b, t, 0)),
                  pl.BlockSpec((tm, D // 2), lambda b, t: (b * nt + t, 0)),
                  pl.BlockSpec((tm, D // 2), lambda b, t: (nblk + b * nt + t, 0)),
                  pl.BlockSpec((1, 1, D), lambda b, t: (b, 0, 0)), pl.BlockSpec((1, D), lambda b, t: (0, 0))],
        out_specs=pl.BlockSpec((1, tm, D), lambda b, t: (b, t, 0)),
        compiler_params=_cparams(("parallel", "parallel")),
        name="final",
    )(h1, y01, y01, g2[:, None, :], final_g.reshape(1, D))


def _batch_chunks(n_batch):
    if n_batch < 4 or n_batch % 2:
        return [1] * n_batch
    return [1] + [2] * (n_batch // 2 - 1) + [1]


def kernel(x, c, w_mod, b_mod, norm1_g, w_in, conv_w, conv_b, w_rg_a, b_rg_a, w_rg_x, b_rg_x, lru_lambda, kv_norm_g, w_uk, w_uv, w_rnn_out, w_att_out, w_out, norm2_g, w_group, b_group, w_expert, b_expert, moe_w1, moe_w3, moe_w2, final_g):
    B, S, D = x.shape
    depth = w_mod.shape[0]
    d_rnn = conv_w.shape[2]
    kv_rank, _, head_dim = w_uk.shape[1:]
    d_att = N_HEADS * head_dim
    idx_dim = (w_in.shape[2] - 2 * d_rnn - d_att - kv_rank - IDX_HEADS - 2 * D) // (IDX_HEADS + 1)
    topk = min(TOPK_MAX, S // 4)
    idx_scale = idx_dim ** -0.5 * IDX_HEADS ** -0.5
    att_scale = head_dim ** -0.5
    h = x
    for l in range(depth):
        mod = _adaln(c, w_mod[l], b_mod[l])
        sh1, sc1, g1, sh2, sc2, g2 = jnp.split(mod, 6, axis=-1)
        xr, gr, qlat, ctab, qi, ki, wi, ga, gb = _inproj(h, sh1, sc1, norm1_g[l], w_in[l], w_uk[l], kv_norm_g[l],
                                                        (d_rnn, d_att, kv_rank, idx_dim))
        s_pad = -(-S // COUNT_TILE) * COUNT_TILE
        if s_pad != S:
            ki = jnp.pad(ki, ((0, 0), (0, s_pad - S), (0, 0)))
        table = ctab.reshape(B * S, kv_rank // 2)
        qlat2 = qlat.reshape(B * S, -1)
        olat, b0, waiting, hg = None, 0, [], None
        for step, nb in enumerate(_batch_chunks(B)):
            scores, params = _indexer(qi, ki, wi, topk, idx_scale, b0, nb, olat)
            waiting.append(_select_gather(scores, params, table, topk, S, b0) + (b0,))
            tie = None
            if step == 2:
                hg = _rglru(xr, gr, conv_w[l], conv_b[l], w_rg_a[l], b_rg_a[l], w_rg_x[l], b_rg_x[l],
                            lru_lambda[l], after=params)
                tie = hg
            if len(waiting) > 1:
                sel, csel, bq = waiting.pop(0)
                olat = _sparse_attn(csel, sel, qlat2, S, topk, att_scale, bq * S, olat, after=tie)
            b0 += nb
        if hg is None:
            hg = _rglru(xr, gr, conv_w[l], conv_b[l], w_rg_a[l], b_rg_a[l], w_rg_x[l], b_rg_x[l], lru_lambda[l])
        for sel, csel, bq in waiting:
            olat = _sparse_attn(csel, sel, qlat2, S, topk, att_scale, bq * S, olat)
        h1, xrow, route = _merge(olat, hg, ga, gb, h, g1, sh2, sc2, norm2_g[l], w_uv[l], w_att_out[l],
                                 w_rnn_out[l], w_out[l], w_group[l], b_group[l], w_expert[l], b_expert[l])
        n_exp = moe_w1.shape[1]
        tmm = MOE_TILE
        a_pad, dest, tile_expert, n_valid = _moe_dispatch(route.reshape(B * S, LANES)[:, :2], n_exp, tmm,
                                                          SC_GATHER_GRANULE)
        xs = _sc_row_scatter2(xrow.reshape(B * S, -1), dest, a_pad)
        yp = _moe_grouped(xs, tile_expert, n_valid, moe_w1[l], moe_w3[l], moe_w2[l], tmm)
        y01 = _sc_row_gather(yp, dest.reshape(2 * B * S))
        h = _final(h1, y01, g2, final_g, final_norm=(l == depth - 1))
    return h
```
